```python
import math
import jax
import jax.numpy as jnp
from jax import lax
import numpy as np

D_MODEL = 2048
BATCH = 2
SEQ = 8192
DEPTH = 1

CHUNK = 64
EPS = 1e-6
D_SSM = D_MODEL // 2
SSM_GROUP_CH = 16
N_SSM_GROUPS = D_SSM // SSM_GROUP_CH
SSM_STATE = 64
DT_MIN = 1e-3
DT_MAX = 1e-1
LAMBDA_RE_MAX = -1e-4
QK_NOPE = 128
QK_ROPE = 64
QK_DIM = QK_NOPE + QK_ROPE
V_HEAD = 128
MLA_HEADS = (D_MODEL // 2) // V_HEAD
Q_LORA = D_MODEL // 4
KV_LORA = D_MODEL // 8
ROPE_THETA = 10000.0
Q_BLOCK = 128
SPLIT_POINTS = (D_SSM, D_SSM + Q_LORA, D_SSM + Q_LORA + KV_LORA, D_SSM + Q_LORA + KV_LORA + QK_ROPE, D_SSM + Q_LORA + KV_LORA + QK_ROPE + D_MODEL)
D_IN = D_SSM + Q_LORA + KV_LORA + QK_ROPE + 2 * D_MODEL
N_EXPERTS = 64
TOP_K = 8
N_EXPERT_GROUPS = 8
TOPK_GROUPS = 4
D_EXPERT = D_MODEL // 4
D_SHARED = D_MODEL // 4
ROUTED_SCALE = 2.5
MOE_BLOCK = 128

kernel_name = 'hybrid_s5_mla_moe_block'


def rms_norm(x, g):
    xf = x.astype(jnp.float32)
    y = xf * lax.rsqrt(jnp.mean(xf * xf, axis=-1, keepdims=True) + EPS)
    return (y * g.astype(jnp.float32)).astype(x.dtype)


def _complex_affine_combine(e1, e2):
    a1r, a1i, b1r, b1i = e1
    a2r, a2i, b2r, b2i = e2
    return (a1r * a2r - a1i * a2i,
            a1r * a2i + a1i * a2r,
            a2r * b1r - a2i * b1i + b2r,
            a2r * b1i + a2i * b1r + b2i)


def s5_mixer(u, a_re, a_im, log_dt, b_re, b_im, c_re, c_im, d):
    f32 = jnp.float32
    bn, s_len, _ = u.shape
    uf = u.astype(f32).reshape(bn, s_len, N_SSM_GROUPS, SSM_GROUP_CH)
    dt = jnp.exp(log_dt.astype(f32))[:, None]
    lam_re = jnp.minimum(a_re.astype(f32), LAMBDA_RE_MAX)
    lam_im = a_im.astype(f32)
    mag = jnp.exp(lam_re * dt)
    ang = lam_im * dt
    lb_re = mag * jnp.cos(ang)
    lb_im = mag * jnp.sin(ang)
    den = lam_re * lam_re + lam_im * lam_im
    num_re = lb_re - 1.0
    coef_re = (num_re * lam_re + lb_im * lam_im) / den
    coef_im = (lb_im * lam_re - num_re * lam_im) / den
    br = b_re.astype(f32)
    bi = b_im.astype(f32)
    bb_re = coef_re[..., None] * br - coef_im[..., None] * bi
    bb_im = coef_re[..., None] * bi + coef_im[..., None] * br
    bu_re = jnp.einsum('bsgc,gnc->bsgn', uf, bb_re)
    bu_im = jnp.einsum('bsgc,gnc->bsgn', uf, bb_im)
    a_seq_re = jnp.broadcast_to(lb_re, (1, s_len, N_SSM_GROUPS, SSM_STATE))
    a_seq_im = jnp.broadcast_to(lb_im, (1, s_len, N_SSM_GROUPS, SSM_STATE))
    _, _, st_re, st_im = lax.associative_scan(_complex_affine_combine, (a_seq_re, a_seq_im, bu_re, bu_im), axis=1)
    y = (jnp.einsum('bsgn,gcn->bsgc', st_re, c_re.astype(f32))
         - jnp.einsum('bsgn,gcn->bsgc', st_im, c_im.astype(f32))
         + d.astype(f32).reshape(N_SSM_GROUPS, SSM_GROUP_CH) * uf)
    return jax.nn.gelu(y).reshape(bn, s_len, D_SSM).astype(u.dtype)


def rope_cos_sin(positions):
    inv = ROPE_THETA ** (-jnp.arange(QK_ROPE // 2, dtype=jnp.float32) * (2.0 / QK_ROPE))
    ang = positions.astype(jnp.float32)[..., None] * inv
    return jnp.cos(ang)[:, :, None, :], jnp.sin(ang)[:, :, None, :]


def apply_rope(x, cos, sin):
    xf = x.astype(jnp.float32)
    half = QK_ROPE // 2
    x1, x2 = xf[..., :half], xf[..., half:]
    return jnp.concatenate([x1 * cos - x2 * sin, x2 * cos + x1 * sin], axis=-1).astype(x.dtype)


def mla_mixer(q_lat, kv_lat, k_pe, positions, q_lat_g, w_q_up, kv_lat_g, w_kv_up, q_norm_g, k_norm_g):
    bn, s_len, _ = q_lat.shape
    q = (rms_norm(q_lat, q_lat_g) @ w_q_up).reshape(bn, s_len, MLA_HEADS, QK_DIM)
    kv = (rms_norm(kv_lat, kv_lat_g) @ w_kv_up).reshape(bn, s_len, MLA_HEADS, QK_NOPE + V_HEAD)
    k_nope, v = kv[..., :QK_NOPE], kv[..., QK_NOPE:]
    k = jnp.concatenate([k_nope, jnp.broadcast_to(k_pe[:, :, None, :], (bn, s_len, MLA_HEADS, QK_ROPE))], axis=-1)
    q = rms_norm(q, q_norm_g)
    k = rms_norm(k, k_norm_g)
    cos, sin = rope_cos_sin(positions)
    q = jnp.concatenate([q[..., :QK_NOPE], apply_rope(q[..., QK_NOPE:], cos, sin)], axis=-1)
    k = jnp.concatenate([k[..., :QK_NOPE], apply_rope(k[..., QK_NOPE:], cos, sin)], axis=-1)
    q = q * jnp.asarray(QK_DIM ** -0.5, q.dtype)
    n_blk = s_len // Q_BLOCK
    q_blocks = q.reshape(bn, n_blk, Q_BLOCK, MLA_HEADS, QK_DIM).transpose(1, 0, 2, 3, 4)
    key_chunk = jnp.arange(s_len) // CHUNK

    def attend(args):
        q_blk, blk = args
        s = jnp.einsum('bqhd,bkhd->bhqk', q_blk, k, preferred_element_type=jnp.float32)
        q_chunk = (blk * Q_BLOCK + jnp.arange(Q_BLOCK)) // CHUNK
        mask = key_chunk[None, :] <= q_chunk[:, None]
        p = jax.nn.softmax(jnp.where(mask, s, -jnp.inf), axis=-1)
        return jnp.einsum('bhqk,bkhd->bqhd', p.astype(v.dtype), v)

    o = lax.map(attend, (q_blocks, jnp.arange(n_blk)))
    return o.transpose(1, 0, 2, 3, 4).reshape(bn, s_len, MLA_HEADS * V_HEAD)


def routed_experts(hf, top_e, top_w, w_exp_gate, w_exp_up, w_exp_down):
    t, _ = hf.shape
    n_assign = t * TOP_K
    n_blk = -(-n_assign // MOE_BLOCK) + N_EXPERTS
    buf_len = n_blk * MOE_BLOCK
    flat_e = top_e.reshape(-1)
    flat_tok = jnp.repeat(jnp.arange(t, dtype=jnp.int32), TOP_K)
    flat_w = top_w.reshape(-1)
    order = jnp.argsort(flat_e)
    sorted_e = flat_e[order]
    counts = jnp.bincount(flat_e, length=N_EXPERTS)
    start = jnp.cumsum(counts) - counts
    padded = (counts + MOE_BLOCK - 1) // MOE_BLOCK * MOE_BLOCK
    pad_end = jnp.cumsum(padded)
    pad_start = pad_end - padded
    dest = pad_start[sorted_e] + jnp.arange(n_assign) - start[sorted_e]
    tok_buf = jnp.zeros((buf_len,), jnp.int32).at[dest].set(flat_tok[order])
    w_buf = jnp.zeros((buf_len,), jnp.float32).at[dest].set(flat_w[order])
    blk_expert = jnp.minimum(jnp.searchsorted(pad_end, jnp.arange(n_blk) * MOE_BLOCK, side='right'), N_EXPERTS - 1)

    def step(acc, inp):
        tok, w, e = inp
        xb = hf[tok]
        hid = jax.nn.silu(xb @ w_exp_gate[e]) * (xb @ w_exp_up[e])
        y = (hid @ w_exp_down[e]) * w[:, None].astype(hf.dtype)
        return acc.at[tok].add(y), None

    acc, _ = lax.scan(step, jnp.zeros_like(hf), (tok_buf.reshape(n_blk, MOE_BLOCK), w_buf.reshape(n_blk, MOE_BLOCK), blk_expert))
    return acc


def moe(h, w_router, router_bias, w_exp_gate, w_exp_up, w_exp_down, w_sh_gate, w_sh_up, w_sh_down):
    bn, s_len, d = h.shape
    t = bn * s_len
    hf = h.reshape(t, d)
    shared = (jax.nn.silu(hf @ w_sh_gate) * (hf @ w_sh_up)) @ w_sh_down
    scores = jax.nn.sigmoid(hf.astype(jnp.float32) @ w_router.astype(jnp.float32))
    sel = scores + router_bias.astype(jnp.float32)
    per_grp = N_EXPERTS // N_EXPERT_GROUPS
    grp_score = lax.top_k(sel.reshape(t, N_EXPERT_GROUPS, per_grp), 2)[0].sum(-1)
    _, top_grp = lax.top_k(grp_score, TOPK_GROUPS)
    grp_mask = jax.nn.one_hot(top_grp, N_EXPERT_GROUPS, dtype=jnp.float32).sum(1) > 0
    exp_mask = jnp.repeat(grp_mask, per_grp, axis=1)
    _, top_e = lax.top_k(jnp.where(exp_mask, sel, -jnp.inf), TOP_K)
    top_w = jnp.take_along_axis(scores, top_e, axis=1)
    top_w = top_w / jnp.sum(top_w, axis=-1, keepdims=True) * ROUTED_SCALE
    routed = routed_experts(hf, top_e, top_w, w_exp_gate, w_exp_up, w_exp_down)
    return (shared + routed).reshape(bn, s_len, d)


def hybrid_layer(x, positions, norm1_g, w_in, q_lat_g, w_q_up, kv_lat_g, w_kv_up, q_norm_g, k_norm_g,
                 ssm_a_re, ssm_a_im, ssm_log_dt, ssm_b_re, ssm_b_im, ssm_c_re, ssm_c_im, ssm_d,
                 w_ssm_val, w_ssm_gate, w_mla_proj, w_out, norm2_g, w_router, router_bias,
                 w_exp_gate, w_exp_up, w_exp_down, w_sh_gate, w_sh_up, w_sh_down):
    h = rms_norm(x, norm1_g)
    proj = h @ w_in
    u, q_lat, kv_lat, k_pe, g_ssm, g_mla = jnp.split(proj, SPLIT_POINTS, axis=-1)
    y_ssm = s5_mixer(u, ssm_a_re, ssm_a_im, ssm_log_dt, ssm_b_re, ssm_b_im, ssm_c_re, ssm_c_im, ssm_d)
    ssm_branch = (y_ssm @ w_ssm_val) * jax.nn.sigmoid(y_ssm @ w_ssm_gate)
    mla_branch = mla_mixer(q_lat, kv_lat, k_pe, positions, q_lat_g, w_q_up, kv_lat_g, w_kv_up, q_norm_g, k_norm_g) @ w_mla_proj
    merged = jax.nn.sigmoid(g_ssm) * ssm_branch + jax.nn.sigmoid(g_mla) * mla_branch
    x = x + merged @ w_out
    h2 = rms_norm(x, norm2_g)
    return x + moe(h2, w_router, router_bias, w_exp_gate, w_exp_up, w_exp_down, w_sh_gate, w_sh_up, w_sh_down)


def setup_inputs(seed: int = 0) -> dict:
    key = jax.random.key(seed)
    f32 = jnp.float32

    def nk(i):
        return jax.random.fold_in(key, i)

    def dense(i, shape, fan_in):
        return jax.random.normal(nk(i), (DEPTH,) + shape, f32) * fan_in ** -0.5

    def gain(i, n):
        return 1.0 + 0.02 * jax.random.normal(nk(i), (DEPTH, n), f32)

    g, n, c = N_SSM_GROUPS, SSM_STATE, SSM_GROUP_CH
    return {
        'x': jax.random.normal(nk(0), (BATCH, SEQ, D_MODEL), f32),
        'positions': jnp.broadcast_to(jnp.arange(SEQ, dtype=jnp.int32), (BATCH, SEQ)),
        'norm1_g': gain(1, D_MODEL),
        'w_in': dense(2, (D_MODEL, D_IN), D_MODEL),
        'q_lat_g': gain(3, Q_LORA),
        'w_q_up': dense(4, (Q_LORA, MLA_HEADS * QK_DIM), Q_LORA),
        'kv_lat_g': gain(5, KV_LORA),
        'w_kv_up': dense(6, (KV_LORA, MLA_HEADS * (QK_NOPE + V_HEAD)), KV_LORA),
        'q_norm_g': gain(7, QK_DIM),
        'k_norm_g': gain(8, QK_DIM),
        'ssm_a_re': -0.5 + 0.01 * jax.random.normal(nk(9), (DEPTH, g, n), f32),
        'ssm_a_im': math.pi * jnp.arange(n, dtype=f32) + 0.01 * jax.random.normal(nk(10), (DEPTH, g, n), f32),
        'ssm_log_dt': jax.random.uniform(nk(11), (DEPTH, g), f32, math.log(DT_MIN), math.log(DT_MAX)),
        'ssm_b_re': dense(12, (g, n, c), 2 * c),
        'ssm_b_im': dense(13, (g, n, c), 2 * c),
        'ssm_c_re': dense(14, (g, c, n), 2 * n),
        'ssm_c_im': dense(15, (g, c, n), 2 * n),
        'ssm_d': jax.random.normal(nk(16), (DEPTH, D_SSM), f32),
        'w_ssm_val': dense(17, (D_SSM, D_MODEL), D_SSM),
        'w_ssm_gate': dense(18, (D_SSM, D_MODEL), D_SSM),
        'w_mla_proj': dense(19, (MLA_HEADS * V_HEAD, D_MODEL), MLA_HEADS * V_HEAD),
        'w_out': dense(20, (D_MODEL, D_MODEL), D_MODEL),
        'norm2_g': gain(21, D_MODEL),
        'w_router': dense(22, (D_MODEL, N_EXPERTS), D_MODEL),
        'router_bias': 0.01 * jax.random.normal(nk(23), (DEPTH, N_EXPERTS), f32),
        'w_exp_gate': dense(24, (N_EXPERTS, D_MODEL, D_EXPERT), D_MODEL),
        'w_exp_up': dense(25, (N_EXPERTS, D_MODEL, D_EXPERT), D_MODEL),
        'w_exp_down': dense(26, (N_EXPERTS, D_EXPERT, D_MODEL), D_EXPERT),
        'w_sh_gate': dense(27, (D_MODEL, D_SHARED), D_MODEL),
        'w_sh_up': dense(28, (D_MODEL, D_SHARED), D_MODEL),
        'w_sh_down': dense(29, (D_SHARED, D_MODEL), D_SHARED),
    }


def reference(x, positions, norm1_g, w_in, q_lat_g, w_q_up, kv_lat_g, w_kv_up, q_norm_g, k_norm_g,
              ssm_a_re, ssm_a_im, ssm_log_dt, ssm_b_re, ssm_b_im, ssm_c_re, ssm_c_im, ssm_d,
              w_ssm_val, w_ssm_gate, w_mla_proj, w_out, norm2_g, w_router, router_bias,
              w_exp_gate, w_exp_up, w_exp_down, w_sh_gate, w_sh_up, w_sh_down):
    layer_params = (norm1_g, w_in, q_lat_g, w_q_up, kv_lat_g, w_kv_up, q_norm_g, k_norm_g,
                    ssm_a_re, ssm_a_im, ssm_log_dt, ssm_b_re, ssm_b_im, ssm_c_re, ssm_c_im, ssm_d,
                    w_ssm_val, w_ssm_gate, w_mla_proj, w_out, norm2_g, w_router, router_bias,
                    w_exp_gate, w_exp_up, w_exp_down, w_sh_gate, w_sh_up, w_sh_down)
    for layer in range(DEPTH):
        x = hybrid_layer(x, positions, *[p[layer] for p in layer_params])
    return x
```

```python
import functools

import jax
import jax.numpy as jnp
from jax import lax
from jax.experimental import pallas as pl
from jax.experimental.pallas import tpu as pltpu

f32 = jnp.float32
bf16 = jnp.bfloat16

D_MODEL = 2048
CHUNK = 64
EPS = 1e-6
D_SSM = 1024
SSM_GROUP_CH = 16
N_SSM_GROUPS = 64
SSM_STATE = 64
LAMBDA_RE_MAX = -1e-4
QK_NOPE = 128
QK_ROPE = 64
QK_DIM = QK_NOPE + QK_ROPE
V_HEAD = 128
MLA_HEADS = 8
Q_LORA = 512
KV_LORA = 256
ROPE_THETA = 10000.0
N_EXPERTS = 64
TOP_K = 8
N_EXPERT_GROUPS = 8
TOPK_GROUPS = 4
D_EXPERT = 512
D_SHARED = 512
ROUTED_SCALE = 2.5

LANES = 128
ROW_TILES = D_MODEL // LANES
VMEM_LIMIT = 52 * 1024 * 1024

SSM_L = 8
SSM_SG = 8
SSM_SG_CH = D_SSM // SSM_SG
SSM_ROW = SSM_L * SSM_SG_CH
SSM_HALF = SSM_SG_CH // SSM_GROUP_CH * SSM_STATE

OFF_GS, OFF_GM, OFF_U, OFF_QL, OFF_KV, OFF_KPE = 0, 2048, 4096, 5120, 5632, 5888
N_PROJ = 6144

MOE_BM = 256


def _cparams(sem, vmem=VMEM_LIMIT):
    return pltpu.CompilerParams(dimension_semantics=sem, vmem_limit_bytes=vmem)


def _gelu_tanh(x):
    return 0.5 * x * (1.0 + jnp.tanh(0.7978845608028654 * (x + 0.044715 * (x * x * x))))


def _norm_proj_kernel(x_ref, g_ref, w_ref, o_ref, h_ref):
    @pl.when(pl.program_id(1) == 0)
    def _():
        x = x_ref[...]
        ms = jnp.mean(x * x, axis=-1, keepdims=True)
        h_ref[...] = (x * lax.rsqrt(ms + EPS) * g_ref[...]).astype(bf16)

    o_ref[...] = jnp.dot(h_ref[...], w_ref[...], preferred_element_type=f32).astype(o_ref.dtype)


def _norm_proj(x2d, g, w, tm, tn):
    t, d = x2d.shape
    n = w.shape[1]
    return pl.pallas_call(
        _norm_proj_kernel,
        out_shape=jax.ShapeDtypeStruct((t, n), bf16),
        grid=(t // tm, n // tn),
        in_specs=[pl.BlockSpec((tm, d), lambda i, j: (i, 0)),
                  pl.BlockSpec((1, d), lambda i, j: (0, 0)),
                  pl.BlockSpec((d, tn), lambda i, j: (0, j))],
        out_specs=pl.BlockSpec((tm, tn), lambda i, j: (i, j)),
        scratch_shapes=[pltpu.VMEM((tm, d), bf16)],
        compiler_params=_cparams(("parallel", "arbitrary")),
        name="norm_in_proj",
    )(x2d, g, w)


def _ssm_weights(a_re, a_im, log_dt, b_re, b_im, c_re, c_im):
    hi = lax.Precision.HIGHEST
    g, n = a_re.shape
    c = b_re.shape[2]
    gl = g // SSM_SG
    dt = jnp.exp(log_dt)[:, None]
    lam_re = jnp.minimum(a_re, LAMBDA_RE_MAX)
    lam_im = a_im
    mag = jnp.exp(lam_re * dt)
    ang = lam_im * dt
    lb_re = mag * jnp.cos(ang)
    lb_im = mag * jnp.sin(ang)
    den = lam_re * lam_re + lam_im * lam_im
    num_re = lb_re - 1.0
    coef_re = (num_re * lam_re + lb_im * lam_im) / den
    coef_im = (lb_im * lam_re - num_re * lam_im) / den
    bb_re = coef_re[..., None] * b_re - coef_im[..., None] * b_im
    bb_im = coef_re[..., None] * b_im + coef_im[..., None] * b_re
    pr, pi = [jnp.ones_like(lb_re)], [jnp.zeros_like(lb_re)]
    for _ in range(SSM_L):
        pr_n = pr[-1] * lb_re - pi[-1] * lb_im
        pi_n = pr[-1] * lb_im + pi[-1] * lb_re
        pr.append(pr_n)
        pi.append(pi_n)
    pw_re = jnp.stack(pr)
    pw_im = jnp.stack(pi)
    eye = jnp.eye(gl, dtype=f32)

    xb_re = pw_re[:SSM_L, :, :, None] * bb_re[None] - pw_im[:SSM_L, :, :, None] * bb_im[None]
    xb_im = pw_re[:SSM_L, :, :, None] * bb_im[None] + pw_im[:SSM_L, :, :, None] * bb_re[None]

    k = (jnp.einsum('gcn,lgnd->lgdc', c_re, xb_re, precision=hi)
         - jnp.einsum('gcn,lgnd->lgdc', c_im, xb_im, precision=hi))
    kbd = jnp.einsum('lsgdc,gh->lsgdhc', k.reshape(SSM_L, SSM_SG, gl, c, c), eye)
    kbd = kbd.reshape(SSM_L, SSM_SG, SSM_SG_CH, SSM_SG_CH)
    kbd = jnp.concatenate([kbd, jnp.zeros_like(kbd[:1])], axis=0)
    jj = jnp.arange(SSM_L)[:, None]
    tt = jnp.arange(SSM_L)[None, :]
    lag = jnp.where(tt >= jj, tt - jj, SSM_L)
    m = kbd[lag]
    m = m.transpose(2, 0, 3, 1, 4).reshape(SSM_SG, SSM_ROW, SSM_ROW)

    def state_proj(xb):
        xr = xb[::-1].reshape(SSM_L, SSM_SG, gl, n, c).transpose(1, 0, 2, 4, 3)
        return jnp.einsum('sjgcn,gh->sjgchn', xr, eye)
    p = jnp.stack([state_proj(xb_re), state_proj(xb_im)], axis=4)
    p = p.reshape(SSM_SG, SSM_ROW, 2 * SSM_HALF)

    w_re = pw_re[1:]
    w_im = pw_im[1:]
    q_re = c_re[None] * w_re[:, :, None, :] - c_im[None] * w_im[:, :, None, :]
    q_im = -(c_re[None] * w_im[:, :, None, :] + c_im[None] * w_re[:, :, None, :])

    def out_proj(qp):
        qr = qp.reshape(SSM_L, SSM_SG, gl, c, n).transpose(1, 2, 4, 0, 3)
        return jnp.einsum('sgntc,gh->sgnthc', qr, eye)
    q = jnp.stack([out_proj(q_re), out_proj(q_im)], axis=1)
    q = q.reshape(SSM_SG, 2 * SSM_HALF, SSM_ROW)

    w1 = jnp.concatenate([m, p], axis=2).astype(bf16)
    al_re = pw_re[SSM_L].reshape(SSM_SG, 1, SSM_HALF)
    al_im = pw_im[SSM_L].reshape(SSM_SG, 1, SSM_HALF)
    return w1, q.astype(bf16), al_re, al_im


def _ssm_a_kernel(u_ref, w_ref, yi_ref, sl_ref):
    r = jnp.dot(u_ref[...], w_ref[...], preferred_element_type=f32)
    yi_ref[...] = r[:, :SSM_ROW]
    sl_ref[...] = r[:, SSM_ROW:]


def _ssm_scan_kernel(sl_ref, ar_ref, ai_ref, sp_ref, st_ref):
    nb, tr, _ = sl_ref.shape

    @pl.when(pl.program_id(1) == 0)
    def _():
        st_ref[...] = jnp.zeros(st_ref.shape, f32)

    ar = ar_ref[...]
    ai = ai_ref[...]

    def body(k, carry):
        new = []
        for b in range(nb):
            sr, si = carry[b]
            sp_ref[b, pl.ds(k, 1), :] = jnp.concatenate([sr, si], axis=1)
            row = sl_ref[b, pl.ds(k, 1), :]
            br = row[:, :SSM_HALF]
            bi = row[:, SSM_HALF:]
            new.append((ar * sr - ai * si + br, ar * si + ai * sr + bi))
        return tuple(new)

    init = tuple((st_ref[b][:, :SSM_HALF], st_ref[b][:, SSM_HALF:]) for b in range(nb))
    fin = lax.fori_loop(0, tr, body, init)
    for b in range(nb):
        st_ref[b] = jnp.concatenate(fin[b], axis=1)


def _ssm_b_kernel(yi_ref, sp_ref, u_ref, q_ref, d_ref, y_ref):
    y = (yi_ref[...]
         + jnp.dot(sp_ref[...].astype(bf16), q_ref[...], preferred_element_type=f32)
         + d_ref[...] * u_ref[...].astype(f32))
    y_ref[...] = _gelu_tanh(y).astype(y_ref.dtype)


def _s5_mixer(u2d, batch, w1, q, al_re, al_im, d):
    t = u2d.shape[0]
    r = t // SSM_L
    rb = r // batch
    tr = min(512, rb)
    u_sg = u2d.reshape(r, SSM_L, SSM_SG, SSM_SG_CH).transpose(2, 0, 1, 3).reshape(SSM_SG, r, SSM_ROW)
    yi, sl = pl.pallas_call(
        _ssm_a_kernel,
        out_shape=(jax.ShapeDtypeStruct((SSM_SG, r, SSM_ROW), f32),
                   jax.ShapeDtypeStruct((SSM_SG, r, 2 * SSM_HALF), f32)),
        grid=(SSM_SG, r // tr),
        in_specs=[pl.BlockSpec((None, tr, SSM_ROW), lambda s, i: (s, i, 0)),
                  pl.BlockSpec((None, SSM_ROW, SSM_ROW + 2 * SSM_HALF), lambda s, i: (s, 0, 0))],
        out_specs=(pl.BlockSpec((None, tr, SSM_ROW), lambda s, i: (s, i, 0)),
                   pl.BlockSpec((None, tr, 2 * SSM_HALF), lambda s, i: (s, i, 0))),
        compiler_params=_cparams(("parallel", "parallel")),
        name="ssm_intra",
    )(u_sg, w1)

    sl4 = sl.reshape(SSM_SG, batch, rb, 2 * SSM_HALF)
    sp4 = pl.pallas_call(
        _ssm_scan_kernel,
        out_shape=jax.ShapeDtypeStruct(sl4.shape, f32),
        grid=(SSM_SG, rb // tr),
        in_specs=[pl.BlockSpec((None, batch, tr, 2 * SSM_HALF), lambda s, i: (s, 0, i, 0)),
                  pl.BlockSpec((None, 1, SSM_HALF), lambda s, i: (s, 0, 0)),
                  pl.BlockSpec((None, 1, SSM_HALF), lambda s, i: (s, 0, 0))],
        out_specs=pl.BlockSpec((None, batch, tr, 2 * SSM_HALF), lambda s, i: (s, 0, i, 0)),
        scratch_shapes=[pltpu.VMEM((batch, 1, 2 * SSM_HALF), f32)],
        compiler_params=_cparams(("parallel", "arbitrary")),
        name="ssm_scan",
    )(sl4, al_re, al_im)
    sp = sp4.reshape(SSM_SG, r, 2 * SSM_HALF)

    d_sg = jnp.tile(d.reshape(SSM_SG, 1, SSM_SG_CH), (1, 1, SSM_L))
    y_sg = pl.pallas_call(
        _ssm_b_kernel,
        out_shape=jax.ShapeDtypeStruct((SSM_SG, r, SSM_ROW), bf16),
        grid=(SSM_SG, r // tr),
        in_specs=[pl.BlockSpec((None, tr, SSM_ROW), lambda s, i: (s, i, 0)),
                  pl.BlockSpec((None, tr, 2 * SSM_HALF), lambda s, i: (s, i, 0)),
                  pl.BlockSpec((None, tr, SSM_ROW), lambda s, i: (s, i, 0)),
                  pl.BlockSpec((None, 2 * SSM_HALF, SSM_ROW), lambda s, i: (s, 0, 0)),
                  pl.BlockSpec((None, 1, SSM_ROW), lambda s, i: (s, 0, 0))],
        out_specs=pl.BlockSpec((None, tr, SSM_ROW), lambda s, i: (s, i, 0)),
        compiler_params=_cparams(("parallel", "parallel")),
        name="ssm_inter",
    )(yi, sp, u_sg, q, d_sg)
    return y_sg.reshape(SSM_SG, r, SSM_L, SSM_SG_CH).transpose(1, 2, 0, 3).reshape(t, D_SSM)


def _mla_prep_kernel(ql_ref, kvp_ref, qg_ref, kvg_ref, wq_ref, wkv_ref, cq_ref, sq_ref, ck_ref, sk_ref,
                     gqn_ref, gkn_ref, q_ref, k_ref, v_ref, hq_ref, hkv_ref, ro_ref, kss_ref):
    tm = ql_ref.shape[0]
    lane = lax.broadcasted_iota(jnp.int32, (tm, LANES), 1)
    rope_lanes = lane < QK_ROPE

    @pl.when(pl.program_id(1) == 0)
    def _():
        ql = ql_ref[...].astype(f32)
        rq = lax.rsqrt(jnp.mean(ql * ql, axis=-1, keepdims=True) + EPS)
        hq_ref[...] = (ql * rq * qg_ref[...]).astype(bf16)
        kvp = kvp_ref[...].astype(f32)
        kv = kvp[:, :KV_LORA]
        rkv = lax.rsqrt(jnp.mean(kv * kv, axis=-1, keepdims=True) + EPS)
        hkv_ref[...] = (kv * rkv * kvg_ref[...]).astype(bf16)
        kx = kvp[:, KV_LORA:KV_LORA + LANES]
        kss_ref[...] = jnp.sum(jnp.where(rope_lanes, kx * kx, 0.0), axis=-1, keepdims=True)
        ro_ref[...] = kx * ck_ref[...] + pltpu.roll(kx, QK_ROPE, 1) * sk_ref[...]

    qx = jnp.dot(hq_ref[...], wq_ref[...], preferred_element_type=f32)
    x1 = qx[:, :QK_NOPE]
    x2 = qx[:, QK_NOPE:]
    ssq = (jnp.sum(x1 * x1, axis=-1, keepdims=True)
           + jnp.sum(jnp.where(rope_lanes, x2 * x2, 0.0), axis=-1, keepdims=True))
    rinv = lax.rsqrt(ssq * (1.0 / QK_DIM) + EPS) * (QK_DIM ** -0.5)
    q_ref[:, :QK_NOPE] = (x1 * gqn_ref[...] * rinv).astype(q_ref.dtype)
    q_ref[:, QK_NOPE:] = ((x2 * cq_ref[...] + pltpu.roll(x2, QK_ROPE, 1) * sq_ref[...]) * rinv).astype(q_ref.dtype)

    kvx = jnp.dot(hkv_ref[...], wkv_ref[...], preferred_element_type=f32)
    kn = kvx[:, :QK_NOPE]
    rk = lax.rsqrt((jnp.sum(kn * kn, axis=-1, keepdims=True) + kss_ref[...]) * (1.0 / QK_DIM) + EPS)
    k_ref[:, :QK_NOPE] = (kn * gkn_ref[...] * rk).astype(k_ref.dtype)
    k_ref[:, QK_NOPE:] = (ro_ref[...] * rk).astype(k_ref.dtype)
    v_ref[...] = kvx[:, QK_NOPE:].astype(v_ref.dtype)


def _attn_kernel(q_ref, k_ref, v_ref, o_ref, m_ref, l_ref, acc_ref, *, tq, tk):
    qi = pl.program_id(2)
    q = q_ref[...]
    m_ref[...] = jnp.full(m_ref.shape, -jnp.inf, f32)
    l_ref[...] = jnp.zeros(l_ref.shape, f32)
    acc_ref[...] = jnp.zeros(acc_ref.shape, f32)

    def step(j, masked):
        start = pl.multiple_of(j * tk, tk)
        k = k_ref[pl.ds(start, tk), :]
        v = v_ref[pl.ds(start, tk), :]
        s = lax.dot_general(q, k, (((1,), (1,)), ((), ())), preferred_element_type=f32)
        if masked:
            q_chunk = (qi * tq + lax.broadcasted_iota(jnp.int32, (tq, tk), 0)) // CHUNK
            k_chunk = (start + lax.broadcasted_iota(jnp.int32, (tq, tk), 1)) // CHUNK
            s = jnp.where(k_chunk <= q_chunk, s, -jnp.inf)
        m_prev = m_ref[...]
        m_new = jnp.maximum(m_prev, jnp.max(s, axis=-1, keepdims=True))
        alpha = jnp.exp(m_prev - m_new)
        p = jnp.exp(s - m_new)
        l_ref[...] = alpha * l_ref[...] + jnp.sum(p, axis=-1, keepdims=True)
        acc_ref[...] = alpha * acc_ref[...] + jnp.dot(p.astype(v.dtype), v, preferred_element_type=f32)
        m_ref[...] = m_new

    n_full = qi * (tq // tk)

    def full_body(j, c):
        step(j, False)
        return c

    lax.fori_loop(0, n_full, full_body, 0)
    for jj in range(tq // tk):
        step(n_full + jj, True)
    o_ref[...] = (acc_ref[...] / l_ref[...]).astype(o_ref.dtype)


def _mla_mixer(proj, positions, q_lat_g, w_q_up, kv_lat_g, w_kv_up, q_norm_g, k_norm_g, batch, seq):
    t = proj.shape[0]
    half = QK_ROPE // 2
    inv = ROPE_THETA ** (-jnp.arange(half, dtype=f32) * (2.0 / QK_ROPE))
    ang = positions.reshape(t).astype(f32)[:, None] * inv
    cos, sin = jnp.cos(ang), jnp.sin(ang)
    zero = jnp.zeros((t, 2 * half), f32)

    def rope_tables(gain):
        g1, g2 = gain[QK_NOPE:QK_NOPE + half], gain[QK_NOPE + half:]
        return (jnp.concatenate([g1 * cos, g2 * cos, zero], axis=1),
                jnp.concatenate([-g2 * sin, g1 * sin, zero], axis=1))

    cq, sq = rope_tables(q_norm_g)
    ck, sk = rope_tables(k_norm_g)
    wq = w_q_up.reshape(Q_LORA, MLA_HEADS, QK_DIM).transpose(1, 0, 2)
    r1, r2 = wq[..., QK_NOPE:QK_NOPE + half], wq[..., QK_NOPE + half:]
    wq = jnp.concatenate([wq, r2, r1], axis=-1).astype(bf16)
    wkv = w_kv_up.reshape(KV_LORA, MLA_HEADS, QK_NOPE + V_HEAD).transpose(1, 0, 2).astype(bf16)

    tm = min(512, seq)
    nsb = seq // tm
    row = lambda i, h: (i, 0)
    const = lambda i, h: (0, 0)
    head_w = lambda i, h: (h, 0, 0)
    head_o = lambda i, h: (i // nsb, h, i % nsb, 0)
    q, k, v = pl.pallas_call(
        _mla_prep_kernel,
        out_shape=(jax.ShapeDtypeStruct((batch, MLA_HEADS, seq, 2 * LANES), bf16),
                   jax.ShapeDtypeStruct((batch, MLA_HEADS, seq, 2 * LANES), bf16),
                   jax.ShapeDtypeStruct((batch, MLA_HEADS, seq, V_HEAD), bf16)),
        grid=(t // tm, MLA_HEADS),
        in_specs=[pl.BlockSpec((tm, Q_LORA), lambda i, h: (i, OFF_QL // Q_LORA)),
                  pl.BlockSpec((tm, 512), lambda i, h: (i, OFF_KV // 512)),
                  pl.BlockSpec((1, Q_LORA), const),
                  pl.BlockSpec((1, KV_LORA), const),
                  pl.BlockSpec((None, Q_LORA, 2 * LANES), head_w),
                  pl.BlockSpec((None, KV_LORA, QK_NOPE + V_HEAD), head_w),
                  pl.BlockSpec((tm, LANES), row), pl.BlockSpec((tm, LANES), row),
                  pl.BlockSpec((tm, LANES), row), pl.BlockSpec((tm, LANES), row),
                  pl.BlockSpec((1, QK_NOPE), const), pl.BlockSpec((1, QK_NOPE), const)],
        out_specs=(pl.BlockSpec((None, None, tm, 2 * LANES), head_o),
                   pl.BlockSpec((None, None, tm, 2 * LANES), head_o),
                   pl.BlockSpec((None, None, tm, V_HEAD), head_o)),
        scratch_shapes=[pltpu.VMEM((tm, Q_LORA), bf16), pltpu.VMEM((tm, KV_LORA), bf16),
                        pltpu.VMEM((tm, LANES), f32), pltpu.VMEM((tm, 1), f32)],
        compiler_params=_cparams(("parallel", "arbitrary")),
        name="mla_prep",
    )(proj, proj, q_lat_g.reshape(1, -1), kv_lat_g.reshape(1, -1), wq, wkv, cq, sq, ck, sk,
      q_norm_g[:QK_NOPE].reshape(1, -1), k_norm_g[:QK_NOPE].reshape(1, -1))

    tq = min(512, seq)
    tk = tq
    o = pl.pallas_call(
        functools.partial(_attn_kernel, tq=tq, tk=tk),
        out_shape=jax.ShapeDtypeStruct((batch, seq, MLA_HEADS * V_HEAD), bf16),
        grid=(batch, MLA_HEADS, seq // tq),
        in_specs=[pl.BlockSpec((None, None, tq, 2 * LANES), lambda b, h, i: (b, h, i, 0)),
                  pl.BlockSpec((None, None, seq, 2 * LANES), lambda b, h, i: (b, h, 0, 0)),
                  pl.BlockSpec((None, None, seq, V_HEAD), lambda b, h, i: (b, h, 0, 0))],
        out_specs=pl.BlockSpec((None, tq, V_HEAD), lambda b, h, i: (b, i, h)),
        scratch_shapes=[pltpu.VMEM((tq, 1), f32), pltpu.VMEM((tq, 1), f32), pltpu.VMEM((tq, V_HEAD), f32)],
        compiler_params=_cparams(("parallel", "parallel", "arbitrary")),
        name="mla_attention",
    )(q, k, v)
    return o.reshape(t, MLA_HEADS * V_HEAD)


def _mix_kernel(y_ref, gs_ref, o_ref, gm_ref, x_ref, wv_ref, wg_ref, wp_ref, wo_ref, out_ref):
    y = y_ref[...]
    val = jnp.dot(y, wv_ref[...], preferred_element_type=f32)
    gate = jnp.dot(y, wg_ref[...], preferred_element_type=f32)
    ssm = val * jax.nn.sigmoid(gate) * jax.nn.sigmoid(gs_ref[...].astype(f32))
    mla = jnp.dot(o_ref[...], wp_ref[...], preferred_element_type=f32) * jax.nn.sigmoid(gm_ref[...].astype(f32))
    merged = (ssm + mla).astype(bf16)
    out_ref[...] = x_ref[...] + jnp.dot(merged, wo_ref[...], preferred_element_type=f32)


def _mix(y, proj, o, x2d, wv, wg, wp, wo, tm):
    t = x2d.shape[0]
    row = lambda i: (i, 0)
    const = lambda i: (0, 0)
    resident = lambda shape: pl.BlockSpec(shape, const, pipeline_mode=pl.Buffered(1))
    return pl.pallas_call(
        _mix_kernel,
        out_shape=jax.ShapeDtypeStruct((t, D_MODEL), f32),
        grid=(t // tm,),
        in_specs=[pl.BlockSpec((tm, D_SSM), row),
                  pl.BlockSpec((tm, D_MODEL), lambda i: (i, OFF_GS // D_MODEL)),
                  pl.BlockSpec((tm, MLA_HEADS * V_HEAD), row),
                  pl.BlockSpec((tm, D_MODEL), lambda i: (i, OFF_GM // D_MODEL)),
                  pl.BlockSpec((tm, D_MODEL), row),
                  resident(wv.shape), resident(wg.shape), resident(wp.shape), resident(wo.shape)],
        out_specs=pl.BlockSpec((tm, D_MODEL), row),
        compiler_params=_cparams(("parallel",)),
        name="merge_out_proj",
    )(y, proj, o, proj, x2d, wv, wg, wp, wo)


def _moe_pre_kernel(x_ref, g_ref, wr_ref, rb_ref, wsg_ref, wsu_ref, wsd_ref, h2_ref, base_ref, te_ref, tw_ref):
    x = x_ref[...]
    tm = x.shape[0]
    h2 = x * lax.rsqrt(jnp.mean(x * x, axis=-1, keepdims=True) + EPS) * g_ref[...]
    for j in range(ROW_TILES):
        h2_ref[:, j, :] = h2[:, j * LANES:(j + 1) * LANES]
    hb = h2.astype(bf16)
    hid = (jax.nn.silu(jnp.dot(hb, wsg_ref[...], preferred_element_type=f32))
           * jnp.dot(hb, wsu_ref[...], preferred_element_type=f32))
    base_ref[...] = x + jnp.dot(hid.astype(bf16), wsd_ref[...], preferred_element_type=f32)

    logits = jnp.dot(h2, wr_ref[...], preferred_element_type=f32, precision=lax.Precision.HIGHEST)
    scores = jax.nn.sigmoid(logits)
    sel = scores + rb_ref[...]
    neg = -jnp.inf
    per_grp = N_EXPERTS // N_EXPERT_GROUPS
    lane = lax.broadcasted_iota(jnp.int32, (tm, N_EXPERTS), 1).astype(f32)
    grp = jnp.floor(lane * (1.0 / per_grp))

    def first_max(vals, ids, sentinel):
        mx = jnp.max(vals, axis=-1, keepdims=True)
        return mx, jnp.min(jnp.where(vals == mx, ids, sentinel), axis=-1, keepdims=True)

    gscore = jnp.zeros((tm, N_EXPERTS), f32)
    for g in range(N_EXPERT_GROUPS):
        in_g = grp == float(g)
        mg = jnp.where(in_g, sel, neg)
        m1, i1 = first_max(mg, lane, float(N_EXPERTS))
        m2 = jnp.max(jnp.where(lane == i1, neg, mg), axis=-1, keepdims=True)
        gscore = jnp.where(in_g, m1 + m2, gscore)
    allowed = jnp.zeros((tm, N_EXPERTS), f32)
    for _ in range(TOPK_GROUPS):
        _, gi = first_max(gscore, grp, float(N_EXPERT_GROUPS))
        hit = grp == gi
        allowed = jnp.where(hit, 1.0, allowed)
        gscore = jnp.where(hit, neg, gscore)
    masked = jnp.where(allowed > 0.0, sel, neg)
    out_lane = lax.broadcasted_iota(jnp.int32, (tm, LANES), 1)
    te = jnp.zeros((tm, LANES), f32)
    tw = jnp.zeros((tm, LANES), f32)
    wsum = jnp.zeros((tm, 1), f32)
    for kk in range(TOP_K):
        _, idx = first_max(masked, lane, float(N_EXPERTS))
        hit = lane == idx
        wk = jnp.sum(jnp.where(hit, scores, 0.0), axis=-1, keepdims=True)
        masked = jnp.where(hit, neg, masked)
        te = jnp.where(out_lane == kk, idx, te)
        tw = jnp.where(out_lane == kk, wk, tw)
        wsum = wsum + wk
    te_ref[...] = te.astype(jnp.int32)
    tw_ref[...] = tw / wsum * ROUTED_SCALE


def _moe_pre(x1, g2, w_router, router_bias, wsg, wsu, wsd, tm):
    t = x1.shape[0]
    row = lambda i: (i, 0)
    const = lambda i: (0, 0)
    return pl.pallas_call(
        _moe_pre_kernel,
        out_shape=(jax.ShapeDtypeStruct((t, ROW_TILES, LANES), f32),
                   jax.ShapeDtypeStruct((t, D_MODEL), f32),
                   jax.ShapeDtypeStruct((t, LANES), jnp.int32),
                   jax.ShapeDtypeStruct((t, LANES), f32)),
        grid=(t // tm,),
        in_specs=[pl.BlockSpec((tm, D_MODEL), row),
                  pl.BlockSpec((1, D_MODEL), const),
                  pl.BlockSpec(w_router.shape, const),
                  pl.BlockSpec((1, N_EXPERTS), const),
                  pl.BlockSpec(wsg.shape, const), pl.BlockSpec(wsu.shape, const), pl.BlockSpec(wsd.shape, const)],
        out_specs=(pl.BlockSpec((tm, ROW_TILES, LANES), lambda i: (i, 0, 0)),
                   pl.BlockSpec((tm, D_MODEL), row),
                   pl.BlockSpec((tm, LANES), row),
                   pl.BlockSpec((tm, LANES), row)),
        compiler_params=_cparams(("parallel",)),
        name="norm2_router_shared",
    )(x1, g2, w_router, router_bias, wsg, wsu, wsd)


def _routing_tables(top_e, n_blk):
    t = top_e.shape[0]
    onehot = (top_e[:, :, None] == jnp.arange(N_EXPERTS, dtype=jnp.int32)).astype(jnp.int32)
    per_tok = jnp.sum(onehot, axis=1)
    incl = jnp.cumsum(per_tok, axis=0)
    excl = incl - per_tok
    counts = incl[-1]
    padded = (counts + MOE_BM - 1) // MOE_BM * MOE_BM
    pad_end = jnp.cumsum(padded)
    pad_start = pad_end - padded
    pos = jnp.sum(onehot * (excl + pad_start)[:, None, :], axis=-1)
    blk_expert = jnp.minimum(
        jnp.searchsorted(pad_end, jnp.arange(n_blk, dtype=jnp.int32) * MOE_BM, side='right'), N_EXPERTS - 1)
    return pos.reshape(t * TOP_K).astype(jnp.int32), blk_expert.astype(jnp.int32)


def _dispatch_kernel(pos_ref, h2_ref, xs_in_ref, xs_ref, sem):
    del xs_in_ref
    td = h2_ref.shape[0]

    def issue(r, c):
        for kk in range(TOP_K):
            pltpu.make_async_copy(h2_ref.at[r], xs_ref.at[pos_ref[r * TOP_K + kk]], sem).start()
        return c

    def drain(r, c):
        for kk in range(TOP_K):
            pltpu.make_async_copy(h2_ref.at[0], xs_ref.at[0], sem).wait()
        return c

    lax.fori_loop(0, td, issue, 0)
    lax.fori_loop(0, td, drain, 0)


def _expert_kernel(be_ref, xs_ref, wg_ref, wu_ref, wd_ref, y_ref):
    del be_ref
    x = jnp.concatenate([xs_ref[:, j, :] for j in range(ROW_TILES)], axis=1).astype(bf16)
    hid = (jax.nn.silu(jnp.dot(x, wg_ref[...], preferred_element_type=f32))
           * jnp.dot(x, wu_ref[...], preferred_element_type=f32))
    y = jnp.dot(hid.astype(bf16), wd_ref[...], preferred_element_type=f32)
    for j in range(ROW_TILES):
        y_ref[:, j, :] = y[:, j * LANES:(j + 1) * LANES]


def _combine_kernel(pos_ref, base_ref, w_ref, ys_ref, out_ref, buf_ref, sem):
    tc = base_ref.shape[0]

    def issue(r, c):
        for kk in range(TOP_K):
            pltpu.make_async_copy(ys_ref.at[pos_ref[r * TOP_K + kk]], buf_ref.at[kk, r], sem).start()
        return c

    def drain(r, c):
        for kk in range(TOP_K):
            pltpu.make_async_copy(ys_ref.at[0], buf_ref.at[0, 0], sem).wait()
        return c

    lax.fori_loop(0, tc, issue, 0)
    lax.fori_loop(0, tc, drain, 0)
    w = w_ref[...]
    for j in range(ROW_TILES):
        acc = base_ref[:, j * LANES:(j + 1) * LANES]
        for kk in range(TOP_K):
            acc = acc + w[:, kk:kk + 1] * buf_ref[kk, :, j, :]
        out_ref[:, j * LANES:(j + 1) * LANES] = acc


def _routed_experts(h2, base, top_e, top_w, weg, weu, wed):
    t = base.shape[0]
    n_assign = t * TOP_K
    n_blk = -(-n_assign // MOE_BM) + N_EXPERTS
    p_rows = n_blk * MOE_BM
    pos, blk_expert = _routing_tables(top_e[:, :TOP_K], n_blk)

    td = 128
    xs = pl.pallas_call(
        _dispatch_kernel,
        out_shape=jax.ShapeDtypeStruct((p_rows, ROW_TILES, LANES), f32),
        grid=(t // td,),
        in_specs=[pl.BlockSpec((td * TOP_K,), lambda i: (i,), memory_space=pltpu.SMEM),
                  pl.BlockSpec((td, ROW_TILES, LANES), lambda i: (i, 0, 0)),
                  pl.BlockSpec(memory_space=pl.ANY)],
        out_specs=pl.BlockSpec(memory_space=pl.ANY),
        scratch_shapes=[pltpu.SemaphoreType.DMA],
        input_output_aliases={2: 0},
        compiler_params=pltpu.CompilerParams(dimension_semantics=("arbitrary",), vmem_limit_bytes=VMEM_LIMIT,
                                             has_side_effects=True),
        name="moe_dispatch",
    )(pos, h2, jnp.zeros((p_rows, ROW_TILES, LANES), f32))

    ys = pl.pallas_call(
        _expert_kernel,
        out_shape=jax.ShapeDtypeStruct((p_rows, ROW_TILES, LANES), f32),
        grid_spec=pltpu.PrefetchScalarGridSpec(
            num_scalar_prefetch=1,
            grid=(n_blk,),
            in_specs=[pl.BlockSpec((MOE_BM, ROW_TILES, LANES), lambda b, be: (b, 0, 0)),
                      pl.BlockSpec((None, D_MODEL, D_EXPERT), lambda b, be: (be[b], 0, 0)),
                      pl.BlockSpec((None, D_MODEL, D_EXPERT), lambda b, be: (be[b], 0, 0)),
                      pl.BlockSpec((None, D_EXPERT, D_MODEL), lambda b, be: (be[b], 0, 0))],
            out_specs=pl.BlockSpec((MOE_BM, ROW_TILES, LANES), lambda b, be: (b, 0, 0))),
        compiler_params=_cparams(("arbitrary",)),
        name="moe_experts",
    )(blk_expert, xs, weg, weu, wed)

    tc = 128
    return pl.pallas_call(
        _combine_kernel,
        out_shape=jax.ShapeDtypeStruct((t, D_MODEL), f32),
        grid=(t // tc,),
        in_specs=[pl.BlockSpec((tc * TOP_K,), lambda i: (i,), memory_space=pltpu.SMEM),
                  pl.BlockSpec((tc, D_MODEL), lambda i: (i, 0)),
                  pl.BlockSpec((tc, LANES), lambda i: (i, 0)),
                  pl.BlockSpec(memory_space=pl.ANY)],
        out_specs=pl.BlockSpec((tc, D_MODEL), lambda i: (i, 0)),
        scratch_shapes=[pltpu.VMEM((TOP_K, tc, ROW_TILES, LANES), f32), pltpu.SemaphoreType.DMA],
        compiler_params=_cparams(("arbitrary",)),
        name="moe_combine",
    )(pos, base, top_w, ys)


def _in_proj_weight(w_in):
    u, ql, kv, kpe, gs, gm = jnp.split(
        w_in, (D_SSM, D_SSM + Q_LORA, D_SSM + Q_LORA + KV_LORA, D_SSM + Q_LORA + KV_LORA + QK_ROPE,
               D_SSM + Q_LORA + KV_LORA + QK_ROPE + D_MODEL), axis=1)
    half = QK_ROPE // 2
    kpe_ext = jnp.concatenate([kpe, kpe[:, half:], kpe[:, :half]], axis=1)
    pad = jnp.zeros((w_in.shape[0], N_PROJ - OFF_KPE - 2 * QK_ROPE), w_in.dtype)
    return jnp.concatenate([gs, gm, u, ql, kv, kpe_ext, pad], axis=1).astype(bf16)


def _layer(x, positions, norm1_g, w_in, q_lat_g, w_q_up, kv_lat_g, w_kv_up, q_norm_g, k_norm_g,
           ssm_a_re, ssm_a_im, ssm_log_dt, ssm_b_re, ssm_b_im, ssm_c_re, ssm_c_im, ssm_d,
           w_ssm_val, w_ssm_gate, w_mla_proj, w_out, norm2_g, w_router, router_bias,
           w_exp_gate, w_exp_up, w_exp_down, w_sh_gate, w_sh_up, w_sh_down):
    batch, seq, d = x.shape
    t = batch * seq
    x2d = x.reshape(t, d)
    tm_big = min(1024, t)
    proj = _norm_proj(x2d, norm1_g.reshape(1, d), _in_proj_weight(w_in), tm_big, 1024)

    w1, q, al_re, al_im = _ssm_weights(ssm_a_re, ssm_a_im, ssm_log_dt, ssm_b_re, ssm_b_im, ssm_c_re, ssm_c_im)
    y_ssm = _s5_mixer(proj[:, OFF_U:OFF_U + D_SSM], batch, w1, q, al_re, al_im, ssm_d)

    o = _mla_mixer(proj, positions, q_lat_g, w_q_up, kv_lat_g, w_kv_up, q_norm_g, k_norm_g, batch, seq)

    x1 = _mix(y_ssm, proj, o, x2d, w_ssm_val.astype(bf16), w_ssm_gate.astype(bf16),
              w_mla_proj.astype(bf16), w_out.astype(bf16), min(256, t))

    h2, base, top_e, top_w = _moe_pre(x1, norm2_g.reshape(1, d), w_router, router_bias.reshape(1, -1),
                                      w_sh_gate.astype(bf16), w_sh_up.astype(bf16), w_sh_down.astype(bf16),
                                      min(256, t))
    out = _routed_experts(h2, base, top_e, top_w, w_exp_gate.astype(bf16), w_exp_up.astype(bf16),
                          w_exp_down.astype(bf16))
    return out.reshape(batch, seq, d)


def kernel(x, positions, norm1_g, w_in, q_lat_g, w_q_up, kv_lat_g, w_kv_up, q_norm_g, k_norm_g, ssm_a_re, ssm_a_im, ssm_log_dt, ssm_b_re, ssm_b_im, ssm_c_re, ssm_c_im, ssm_d, w_ssm_val, w_ssm_gate, w_mla_proj, w_out, norm2_g, w_router, router_bias, w_exp_gate, w_exp_up, w_exp_down, w_sh_gate, w_sh_up, w_sh_down):
    layer_params = (norm1_g, w_in, q_lat_g, w_q_up, kv_lat_g, w_kv_up, q_norm_g, k_norm_g,
                    ssm_a_re, ssm_a_im, ssm_log_dt, ssm_b_re, ssm_b_im, ssm_c_re, ssm_c_im, ssm_d,
                    w_ssm_val, w_ssm_gate, w_mla_proj, w_out, norm2_g, w_router, router_bias,
                    w_exp_gate, w_exp_up, w_exp_down, w_sh_gate, w_sh_up, w_sh_down)
    for layer in range(norm1_g.shape[0]):
        x = _layer(x, positions, *[p[layer] for p in layer_params])
    return x
```

```python
import functools

import jax
import jax.numpy as jnp
from jax import lax
from jax.experimental import pallas as pl
from jax.experimental.pallas import tpu as pltpu

f32 = jnp.float32
bf16 = jnp.bfloat16

D_MODEL = 2048
CHUNK = 64
EPS = 1e-6
D_SSM = 1024
SSM_GROUP_CH = 16
N_SSM_GROUPS = 64
SSM_STATE = 64
LAMBDA_RE_MAX = -1e-4
QK_NOPE = 128
QK_ROPE = 64
QK_DIM = QK_NOPE + QK_ROPE
V_HEAD = 128
MLA_HEADS = 8
Q_LORA = 512
KV_LORA = 256
ROPE_THETA = 10000.0
N_EXPERTS = 64
TOP_K = 8
N_EXPERT_GROUPS = 8
TOPK_GROUPS = 4
D_EXPERT = 512
D_SHARED = 512
ROUTED_SCALE = 2.5

LOG2_E = 1.4426950408889634
LANES = 128
ROW_TILES = D_MODEL // LANES
VMEM_LIMIT = 52 * 1024 * 1024

SSM_L = 8
SSM_SG = 8
SSM_SG_CH = D_SSM // SSM_SG
SSM_ROW = SSM_L * SSM_SG_CH
SSM_HALF = SSM_SG_CH // SSM_GROUP_CH * SSM_STATE

OFF_GS, OFF_GM, OFF_U, OFF_QL, OFF_KV, OFF_KPE = 0, 2048, 4096, 5120, 5632, 5888
N_PROJ = 6144

MOE_BM = 256


def _cparams(sem, vmem=VMEM_LIMIT):
    return pltpu.CompilerParams(dimension_semantics=sem, vmem_limit_bytes=vmem)


def _gelu_tanh(x):
    return 0.5 * x * (1.0 + jnp.tanh(0.7978845608028654 * (x + 0.044715 * (x * x * x))))


def _norm_proj_kernel(x_ref, g_ref, w_ref, o_ref, h_ref):
    @pl.when(pl.program_id(1) == 0)
    def _():
        x = x_ref[...]
        ms = jnp.mean(x * x, axis=-1, keepdims=True)
        h_ref[...] = (x * lax.rsqrt(ms + EPS) * g_ref[...]).astype(bf16)

    o_ref[...] = jnp.dot(h_ref[...], w_ref[...], preferred_element_type=f32).astype(o_ref.dtype)


def _norm_proj(x2d, g, w, tm, tn):
    t, d = x2d.shape
    n = w.shape[1]
    return pl.pallas_call(
        _norm_proj_kernel,
        out_shape=jax.ShapeDtypeStruct((t, n), bf16),
        grid=(t // tm, n // tn),
        in_specs=[pl.BlockSpec((tm, d), lambda i, j: (i, 0)),
                  pl.BlockSpec((1, d), lambda i, j: (0, 0)),
                  pl.BlockSpec((d, tn), lambda i, j: (0, j))],
        out_specs=pl.BlockSpec((tm, tn), lambda i, j: (i, j)),
        scratch_shapes=[pltpu.VMEM((tm, d), bf16)],
        compiler_params=_cparams(("parallel", "arbitrary")),
        name="norm_in_proj",
    )(x2d, g, w)


def _ssm_weights(a_re, a_im, log_dt, b_re, b_im, c_re, c_im):
    hi = lax.Precision.HIGHEST
    g, n = a_re.shape
    c = b_re.shape[2]
    gl = g // SSM_SG
    dt = jnp.exp(log_dt)[:, None]
    lam_re = jnp.minimum(a_re, LAMBDA_RE_MAX)
    lam_im = a_im
    mag = jnp.exp(lam_re * dt)
    ang = lam_im * dt
    lb_re = mag * jnp.cos(ang)
    lb_im = mag * jnp.sin(ang)
    den = lam_re * lam_re + lam_im * lam_im
    num_re = lb_re - 1.0
    coef_re = (num_re * lam_re + lb_im * lam_im) / den
    coef_im = (lb_im * lam_re - num_re * lam_im) / den
    bb_re = coef_re[..., None] * b_re - coef_im[..., None] * b_im
    bb_im = coef_re[..., None] * b_im + coef_im[..., None] * b_re
    pr, pi = [jnp.ones_like(lb_re)], [jnp.zeros_like(lb_re)]
    for _ in range(SSM_L):
        pr_n = pr[-1] * lb_re - pi[-1] * lb_im
        pi_n = pr[-1] * lb_im + pi[-1] * lb_re
        pr.append(pr_n)
        pi.append(pi_n)
    pw_re = jnp.stack(pr)
    pw_im = jnp.stack(pi)
    eye = jnp.eye(gl, dtype=f32)

    xb_re = pw_re[:SSM_L, :, :, None] * bb_re[None] - pw_im[:SSM_L, :, :, None] * bb_im[None]
    xb_im = pw_re[:SSM_L, :, :, None] * bb_im[None] + pw_im[:SSM_L, :, :, None] * bb_re[None]

    k = (jnp.einsum('gcn,lgnd->lgdc', c_re, xb_re, precision=hi)
         - jnp.einsum('gcn,lgnd->lgdc', c_im, xb_im, precision=hi))
    kbd = jnp.einsum('lsgdc,gh->lsgdhc', k.reshape(SSM_L, SSM_SG, gl, c, c), eye)
    kbd = kbd.reshape(SSM_L, SSM_SG, SSM_SG_CH, SSM_SG_CH)
    kbd = jnp.concatenate([kbd, jnp.zeros_like(kbd[:1])], axis=0)
    jj = jnp.arange(SSM_L)[:, None]
    tt = jnp.arange(SSM_L)[None, :]
    lag = jnp.where(tt >= jj, tt - jj, SSM_L)
    m = kbd[lag]
    m = m.transpose(2, 0, 3, 1, 4).reshape(SSM_SG, SSM_ROW, SSM_ROW)

    def state_proj(xb):
        xr = xb[::-1].reshape(SSM_L, SSM_SG, gl, n, c).transpose(1, 0, 2, 4, 3)
        return jnp.einsum('sjgcn,gh->sjgchn', xr, eye)
    p = jnp.stack([state_proj(xb_re), state_proj(xb_im)], axis=4)
    p = p.reshape(SSM_SG, SSM_ROW, 2 * SSM_HALF)

    w_re = pw_re[1:]
    w_im = pw_im[1:]
    q_re = c_re[None] * w_re[:, :, None, :] - c_im[None] * w_im[:, :, None, :]
    q_im = -(c_re[None] * w_im[:, :, None, :] + c_im[None] * w_re[:, :, None, :])

    def out_proj(qp):
        qr = qp.reshape(SSM_L, SSM_SG, gl, c, n).transpose(1, 2, 4, 0, 3)
        return jnp.einsum('sgntc,gh->sgnthc', qr, eye)
    q = jnp.stack([out_proj(q_re), out_proj(q_im)], axis=1)
    q = q.reshape(SSM_SG, 2 * SSM_HALF, SSM_ROW)

    w1 = jnp.concatenate([m, p], axis=2).astype(bf16)
    al_re = pw_re[SSM_L].reshape(SSM_SG, 1, SSM_HALF)
    al_im = pw_im[SSM_L].reshape(SSM_SG, 1, SSM_HALF)
    return w1, q.astype(bf16), al_re, al_im


def _ssm_a_kernel(u_ref, w_ref, yi_ref, sl_ref):
    r = jnp.dot(u_ref[...], w_ref[...], preferred_element_type=f32)
    yi_ref[...] = r[:, :SSM_ROW]
    sl_ref[...] = r[:, SSM_ROW:]


def _ssm_scan_kernel(sl_ref, ar_ref, ai_ref, sp_ref, st_ref):
    nb, tr, _ = sl_ref.shape

    @pl.when(pl.program_id(1) == 0)
    def _():
        st_ref[...] = jnp.zeros(st_ref.shape, f32)

    ar = ar_ref[...]
    ai = ai_ref[...]

    def body(k, carry):
        new = []
        for b in range(nb):
            sr, si = carry[b]
            sp_ref[b, pl.ds(k, 1), :] = jnp.concatenate([sr, si], axis=1)
            row = sl_ref[b, pl.ds(k, 1), :]
            br = row[:, :SSM_HALF]
            bi = row[:, SSM_HALF:]
            new.append((ar * sr - ai * si + br, ar * si + ai * sr + bi))
        return tuple(new)

    init = tuple((st_ref[b][:, :SSM_HALF], st_ref[b][:, SSM_HALF:]) for b in range(nb))
    fin = lax.fori_loop(0, tr, body, init)
    for b in range(nb):
        st_ref[b] = jnp.concatenate(fin[b], axis=1)


def _ssm_b_kernel(yi_ref, sp_ref, u_ref, q_ref, d_ref, y_ref):
    y = (yi_ref[...]
         + jnp.dot(sp_ref[...].astype(bf16), q_ref[...], preferred_element_type=f32)
         + d_ref[...] * u_ref[...].astype(f32))
    y_ref[...] = _gelu_tanh(y).astype(y_ref.dtype)


def _s5_mixer(u2d, batch, w1, q, al_re, al_im, d):
    t = u2d.shape[0]
    r = t // SSM_L
    rb = r // batch
    tr = min(512, rb)
    u_sg = u2d.reshape(r, SSM_L, SSM_SG, SSM_SG_CH).transpose(2, 0, 1, 3).reshape(SSM_SG, r, SSM_ROW)
    yi, sl = pl.pallas_call(
        _ssm_a_kernel,
        out_shape=(jax.ShapeDtypeStruct((SSM_SG, r, SSM_ROW), f32),
                   jax.ShapeDtypeStruct((SSM_SG, r, 2 * SSM_HALF), f32)),
        grid=(SSM_SG, r // tr),
        in_specs=[pl.BlockSpec((None, tr, SSM_ROW), lambda s, i: (s, i, 0)),
                  pl.BlockSpec((None, SSM_ROW, SSM_ROW + 2 * SSM_HALF), lambda s, i: (s, 0, 0))],
        out_specs=(pl.BlockSpec((None, tr, SSM_ROW), lambda s, i: (s, i, 0)),
                   pl.BlockSpec((None, tr, 2 * SSM_HALF), lambda s, i: (s, i, 0))),
        compiler_params=_cparams(("parallel", "parallel")),
        name="ssm_intra",
    )(u_sg, w1)

    sl4 = sl.reshape(SSM_SG, batch, rb, 2 * SSM_HALF)
    sp4 = pl.pallas_call(
        _ssm_scan_kernel,
        out_shape=jax.ShapeDtypeStruct(sl4.shape, f32),
        grid=(SSM_SG, rb // tr),
        in_specs=[pl.BlockSpec((None, batch, tr, 2 * SSM_HALF), lambda s, i: (s, 0, i, 0)),
                  pl.BlockSpec((None, 1, SSM_HALF), lambda s, i: (s, 0, 0)),
                  pl.BlockSpec((None, 1, SSM_HALF), lambda s, i: (s, 0, 0))],
        out_specs=pl.BlockSpec((None, batch, tr, 2 * SSM_HALF), lambda s, i: (s, 0, i, 0)),
        scratch_shapes=[pltpu.VMEM((batch, 1, 2 * SSM_HALF), f32)],
        compiler_params=_cparams(("parallel", "arbitrary")),
        name="ssm_scan",
    )(sl4, al_re, al_im)
    sp = sp4.reshape(SSM_SG, r, 2 * SSM_HALF)

    d_sg = jnp.tile(d.reshape(SSM_SG, 1, SSM_SG_CH), (1, 1, SSM_L))
    y_sg = pl.pallas_call(
        _ssm_b_kernel,
        out_shape=jax.ShapeDtypeStruct((SSM_SG, r, SSM_ROW), bf16),
        grid=(SSM_SG, r // tr),
        in_specs=[pl.BlockSpec((None, tr, SSM_ROW), lambda s, i: (s, i, 0)),
                  pl.BlockSpec((None, tr, 2 * SSM_HALF), lambda s, i: (s, i, 0)),
                  pl.BlockSpec((None, tr, SSM_ROW), lambda s, i: (s, i, 0)),
                  pl.BlockSpec((None, 2 * SSM_HALF, SSM_ROW), lambda s, i: (s, 0, 0)),
                  pl.BlockSpec((None, 1, SSM_ROW), lambda s, i: (s, 0, 0))],
        out_specs=pl.BlockSpec((None, tr, SSM_ROW), lambda s, i: (s, i, 0)),
        compiler_params=_cparams(("parallel", "parallel")),
        name="ssm_inter",
    )(yi, sp, u_sg, q, d_sg)
    return y_sg.reshape(SSM_SG, r, SSM_L, SSM_SG_CH).transpose(1, 2, 0, 3).reshape(t, D_SSM)


def _mla_prep_kernel(ql_ref, kvp_ref, qg_ref, kvg_ref, wq_ref, wkv_ref, cq_ref, sq_ref, ck_ref, sk_ref,
                     gqn_ref, gkn_ref, q_ref, kt_ref, v_ref):
    tm = ql_ref.shape[0]
    lane = lax.broadcasted_iota(jnp.int32, (tm, LANES), 1)
    rope_lanes = lane < QK_ROPE

    ql = ql_ref[...].astype(f32)
    hq = (ql * lax.rsqrt(jnp.mean(ql * ql, axis=-1, keepdims=True) + EPS) * qg_ref[...]).astype(bf16)
    kvp = kvp_ref[...].astype(f32)
    kv = kvp[:, :KV_LORA]
    hkv = (kv * lax.rsqrt(jnp.mean(kv * kv, axis=-1, keepdims=True) + EPS) * kvg_ref[...]).astype(bf16)
    kx = kvp[:, KV_LORA:KV_LORA + LANES]
    kss = jnp.sum(jnp.where(rope_lanes, kx * kx, 0.0), axis=-1, keepdims=True)
    ro = kx * ck_ref[...] + pltpu.roll(kx, QK_ROPE, 1) * sk_ref[...]

    for h in range(MLA_HEADS):
        qx = jnp.dot(hq, wq_ref[h], preferred_element_type=f32)
        x1 = qx[:, :QK_NOPE]
        x2 = qx[:, QK_NOPE:]
        ssq = (jnp.sum(x1 * x1, axis=-1, keepdims=True)
               + jnp.sum(jnp.where(rope_lanes, x2 * x2, 0.0), axis=-1, keepdims=True))
        rinv = lax.rsqrt(ssq * (1.0 / QK_DIM) + EPS) * (QK_DIM ** -0.5 * LOG2_E)
        q_ref[h, :, :QK_NOPE] = (x1 * gqn_ref[...] * rinv).astype(q_ref.dtype)
        q_ref[h, :, QK_NOPE:] = ((x2 * cq_ref[...] + pltpu.roll(x2, QK_ROPE, 1) * sq_ref[...])
                                 * rinv).astype(q_ref.dtype)

        kvx = jnp.dot(hkv, wkv_ref[h], preferred_element_type=f32)
        kn = kvx[:, :QK_NOPE]
        rk = lax.rsqrt((jnp.sum(kn * kn, axis=-1, keepdims=True) + kss) * (1.0 / QK_DIM) + EPS)
        k = jnp.concatenate([kn * gkn_ref[...] * rk, ro * rk], axis=1)
        kt_ref[h] = k.T.astype(kt_ref.dtype)
        v_ref[h] = kvx[:, QK_NOPE:].astype(v_ref.dtype)


def _attn_kernel(q_ref, kt_ref, v_ref, o_ref, s_ref, m_ref, l_ref, acc_ref, *, tq, sub):
    qi = pl.program_id(2)
    m_ref[...] = jnp.full(m_ref.shape, -jnp.inf, f32)
    l_ref[...] = jnp.zeros(l_ref.shape, f32)
    acc_ref[...] = jnp.zeros(acc_ref.shape, f32)

    def scores(j):
        start = pl.multiple_of(j * tq, tq)
        return jnp.dot(q_ref[...], kt_ref[:, pl.ds(start, tq)], preferred_element_type=f32)

    def update(i, s, v):
        rows = pl.ds(i * sub, sub)
        m_prev = m_ref[rows, :]
        m_new = jnp.maximum(m_prev, jnp.max(s, axis=-1, keepdims=True))
        alpha = jnp.exp2(m_prev - m_new)
        p = jnp.exp2(s - m_new)
        l_ref[rows, :] = alpha * l_ref[rows, :] + jnp.sum(p, axis=-1, keepdims=True)
        acc_ref[rows, :] = alpha * acc_ref[rows, :] + jnp.dot(p.astype(v.dtype), v, preferred_element_type=f32)
        m_ref[rows, :] = m_new

    s_ref[...] = scores(0)

    def full_body(j, c):
        s_next = scores(j + 1)
        v = v_ref[pl.ds(pl.multiple_of(j * tq, tq), tq), :]
        for i in range(tq // sub):
            update(i, s_ref[pl.ds(i * sub, sub), :], v)
        s_ref[...] = s_next
        return c

    lax.fori_loop(0, qi, full_body, 0)

    start = pl.multiple_of(qi * tq, tq)
    for i in range(tq // sub):
        nk = (i + 1) * sub
        s = s_ref[pl.ds(i * sub, sub), pl.ds(0, nk)]
        q_chunk = (i * sub + lax.broadcasted_iota(jnp.int32, (sub, nk), 0)) // CHUNK
        k_chunk = lax.broadcasted_iota(jnp.int32, (sub, nk), 1) // CHUNK
        update(i, jnp.where(k_chunk <= q_chunk, s, -jnp.inf), v_ref[pl.ds(start, nk), :])
    o_ref[...] = (acc_ref[...] / l_ref[...]).astype(o_ref.dtype)


def _mla_mixer(proj, positions, q_lat_g, w_q_up, kv_lat_g, w_kv_up, q_norm_g, k_norm_g, batch, seq):
    t = proj.shape[0]
    half = QK_ROPE // 2
    inv = ROPE_THETA ** (-jnp.arange(half, dtype=f32) * (2.0 / QK_ROPE))
    ang = positions.reshape(t).astype(f32)[:, None] * inv
    cos, sin = jnp.cos(ang), jnp.sin(ang)
    zero = jnp.zeros((t, 2 * half), f32)

    def rope_tables(gain):
        g1, g2 = gain[QK_NOPE:QK_NOPE + half], gain[QK_NOPE + half:]
        return (jnp.concatenate([g1 * cos, g2 * cos, zero], axis=1),
                jnp.concatenate([-g2 * sin, g1 * sin, zero], axis=1))

    cq, sq = rope_tables(q_norm_g)
    ck, sk = rope_tables(k_norm_g)
    wq = w_q_up.reshape(Q_LORA, MLA_HEADS, QK_DIM).transpose(1, 0, 2)
    r1, r2 = wq[..., QK_NOPE:QK_NOPE + half], wq[..., QK_NOPE + half:]
    wq = jnp.concatenate([wq, r2, r1], axis=-1).astype(bf16)
    wkv = w_kv_up.reshape(KV_LORA, MLA_HEADS, QK_NOPE + V_HEAD).transpose(1, 0, 2).astype(bf16)

    tm = min(512, seq)
    nsb = seq // tm
    row = lambda i: (i, 0)
    const = lambda i: (0, 0)
    const3 = lambda i: (0, 0, 0)
    head_o = lambda i: (i // nsb, 0, i % nsb, 0)
    q, kt, v = pl.pallas_call(
        _mla_prep_kernel,
        out_shape=(jax.ShapeDtypeStruct((batch, MLA_HEADS, seq, 2 * LANES), bf16),
                   jax.ShapeDtypeStruct((batch, MLA_HEADS, 2 * LANES, seq), bf16),
                   jax.ShapeDtypeStruct((batch, MLA_HEADS, seq, V_HEAD), bf16)),
        grid=(t // tm,),
        in_specs=[pl.BlockSpec((tm, Q_LORA), lambda i: (i, OFF_QL // Q_LORA)),
                  pl.BlockSpec((tm, 512), lambda i: (i, OFF_KV // 512)),
                  pl.BlockSpec((1, Q_LORA), const),
                  pl.BlockSpec((1, KV_LORA), const),
                  pl.BlockSpec(wq.shape, const3),
                  pl.BlockSpec(wkv.shape, const3),
                  pl.BlockSpec((tm, LANES), row), pl.BlockSpec((tm, LANES), row),
                  pl.BlockSpec((tm, LANES), row), pl.BlockSpec((tm, LANES), row),
                  pl.BlockSpec((1, QK_NOPE), const), pl.BlockSpec((1, QK_NOPE), const)],
        out_specs=(pl.BlockSpec((None, MLA_HEADS, tm, 2 * LANES), head_o),
                   pl.BlockSpec((None, MLA_HEADS, 2 * LANES, tm), lambda i: (i // nsb, 0, 0, i % nsb)),
                   pl.BlockSpec((None, MLA_HEADS, tm, V_HEAD), head_o)),
        compiler_params=_cparams(("parallel",)),
        name="mla_prep",
    )(proj, proj, q_lat_g.reshape(1, -1), kv_lat_g.reshape(1, -1), wq, wkv, cq, sq, ck, sk,
      q_norm_g[:QK_NOPE].reshape(1, -1), k_norm_g[:QK_NOPE].reshape(1, -1))

    tq = min(512, seq)
    o = pl.pallas_call(
        functools.partial(_attn_kernel, tq=tq, sub=min(128, tq)),
        out_shape=jax.ShapeDtypeStruct((batch, seq, MLA_HEADS * V_HEAD), bf16),
        grid=(batch, MLA_HEADS, seq // tq),
        in_specs=[pl.BlockSpec((None, None, tq, 2 * LANES), lambda b, h, i: (b, h, i, 0)),
                  pl.BlockSpec((None, None, 2 * LANES, seq), lambda b, h, i: (b, h, 0, 0)),
                  pl.BlockSpec((None, None, seq, V_HEAD), lambda b, h, i: (b, h, 0, 0))],
        out_specs=pl.BlockSpec((None, tq, V_HEAD), lambda b, h, i: (b, i, h)),
        scratch_shapes=[pltpu.VMEM((tq, tq), f32), pltpu.VMEM((tq, 1), f32), pltpu.VMEM((tq, 1), f32),
                        pltpu.VMEM((tq, V_HEAD), f32)],
        compiler_params=_cparams(("parallel", "parallel", "arbitrary")),
        name="mla_attention",
    )(q, kt, v)
    return o.reshape(t, MLA_HEADS * V_HEAD)


def _mix_kernel(y_ref, gs_ref, o_ref, gm_ref, x_ref, wv_ref, wg_ref, wp_ref, wo_ref, out_ref):
    y = y_ref[...]
    val = jnp.dot(y, wv_ref[...], preferred_element_type=f32)
    gate = jnp.dot(y, wg_ref[...], preferred_element_type=f32)
    ssm = val * jax.nn.sigmoid(gate) * jax.nn.sigmoid(gs_ref[...].astype(f32))
    mla = jnp.dot(o_ref[...], wp_ref[...], preferred_element_type=f32) * jax.nn.sigmoid(gm_ref[...].astype(f32))
    merged = (ssm + mla).astype(bf16)
    out_ref[...] = x_ref[...] + jnp.dot(merged, wo_ref[...], preferred_element_type=f32)


def _mix(y, proj, o, x2d, wv, wg, wp, wo, tm):
    t = x2d.shape[0]
    row = lambda i: (i, 0)
    const = lambda i: (0, 0)
    resident = lambda shape: pl.BlockSpec(shape, const, pipeline_mode=pl.Buffered(1))
    return pl.pallas_call(
        _mix_kernel,
        out_shape=jax.ShapeDtypeStruct((t, D_MODEL), f32),
        grid=(t // tm,),
        in_specs=[pl.BlockSpec((tm, D_SSM), row),
                  pl.BlockSpec((tm, D_MODEL), lambda i: (i, OFF_GS // D_MODEL)),
                  pl.BlockSpec((tm, MLA_HEADS * V_HEAD), row),
                  pl.BlockSpec((tm, D_MODEL), lambda i: (i, OFF_GM // D_MODEL)),
                  pl.BlockSpec((tm, D_MODEL), row),
                  resident(wv.shape), resident(wg.shape), resident(wp.shape), resident(wo.shape)],
        out_specs=pl.BlockSpec((tm, D_MODEL), row),
        compiler_params=_cparams(("parallel",)),
        name="merge_out_proj",
    )(y, proj, o, proj, x2d, wv, wg, wp, wo)


def _moe_pre_kernel(x_ref, g_ref, wr_ref, rb_ref, wsg_ref, wsu_ref, wsd_ref, h2_ref, base_ref, te_ref, tw_ref):
    x = x_ref[...]
    tm = x.shape[0]
    h2 = x * lax.rsqrt(jnp.mean(x * x, axis=-1, keepdims=True) + EPS) * g_ref[...]
    h2_ref[...] = h2
    hb = h2.astype(bf16)
    hid = (jax.nn.silu(jnp.dot(hb, wsg_ref[...], preferred_element_type=f32))
           * jnp.dot(hb, wsu_ref[...], preferred_element_type=f32))
    base_ref[...] = x + jnp.dot(hid.astype(bf16), wsd_ref[...], preferred_element_type=f32)

    logits = jnp.dot(h2, wr_ref[...], preferred_element_type=f32, precision=lax.Precision.HIGHEST)
    scores = jax.nn.sigmoid(logits)
    sel = scores + rb_ref[...]
    neg = -jnp.inf
    per_grp = N_EXPERTS // N_EXPERT_GROUPS
    lane = lax.broadcasted_iota(jnp.int32, (tm, N_EXPERTS), 1).astype(f32)
    grp = jnp.floor(lane * (1.0 / per_grp))

    def first_max(vals, ids, sentinel):
        mx = jnp.max(vals, axis=-1, keepdims=True)
        return mx, jnp.min(jnp.where(vals == mx, ids, sentinel), axis=-1, keepdims=True)

    gscore = jnp.zeros((tm, N_EXPERTS), f32)
    for g in range(N_EXPERT_GROUPS):
        in_g = grp == float(g)
        mg = jnp.where(in_g, sel, neg)
        m1, i1 = first_max(mg, lane, float(N_EXPERTS))
        m2 = jnp.max(jnp.where(lane == i1, neg, mg), axis=-1, keepdims=True)
        gscore = jnp.where(in_g, m1 + m2, gscore)
    allowed = jnp.zeros((tm, N_EXPERTS), f32)
    for _ in range(TOPK_GROUPS):
        _, gi = first_max(gscore, grp, float(N_EXPERT_GROUPS))
        hit = grp == gi
        allowed = jnp.where(hit, 1.0, allowed)
        gscore = jnp.where(hit, neg, gscore)
    masked = jnp.where(allowed > 0.0, sel, neg)
    out_lane = lax.broadcasted_iota(jnp.int32, (tm, LANES), 1)
    te = jnp.zeros((tm, LANES), f32)
    tw = jnp.zeros((tm, LANES), f32)
    wsum = jnp.zeros((tm, 1), f32)
    for kk in range(TOP_K):
        _, idx = first_max(masked, lane, float(N_EXPERTS))
        hit = lane == idx
        wk = jnp.sum(jnp.where(hit, scores, 0.0), axis=-1, keepdims=True)
        masked = jnp.where(hit, neg, masked)
        te = jnp.where(out_lane == kk, idx, te)
        tw = jnp.where(out_lane == kk, wk, tw)
        wsum = wsum + wk
    te_ref[...] = te.astype(jnp.int32)
    tw_ref[...] = tw / wsum * ROUTED_SCALE


def _moe_pre(x1, g2, w_router, router_bias, wsg, wsu, wsd, tm):
    t = x1.shape[0]
    row = lambda i: (i, 0)
    const = lambda i: (0, 0)
    return pl.pallas_call(
        _moe_pre_kernel,
        out_shape=(jax.ShapeDtypeStruct((t, D_MODEL), f32),
                   jax.ShapeDtypeStruct((t, D_MODEL), f32),
                   jax.ShapeDtypeStruct((t, LANES), jnp.int32),
                   jax.ShapeDtypeStruct((t, LANES), f32)),
        grid=(t // tm,),
        in_specs=[pl.BlockSpec((tm, D_MODEL), row),
                  pl.BlockSpec((1, D_MODEL), const),
                  pl.BlockSpec(w_router.shape, const),
                  pl.BlockSpec((1, N_EXPERTS), const),
                  pl.BlockSpec(wsg.shape, const), pl.BlockSpec(wsu.shape, const), pl.BlockSpec(wsd.shape, const)],
        out_specs=(pl.BlockSpec((tm, D_MODEL), row),
                   pl.BlockSpec((tm, D_MODEL), row),
                   pl.BlockSpec((tm, LANES), row),
                   pl.BlockSpec((tm, LANES), row)),
        compiler_params=_cparams(("parallel",)),
        name="norm2_router_shared",
    )(x1, g2, w_router, router_bias, wsg, wsu, wsd)


def _routing_tables(top_e, n_blk):
    t = top_e.shape[0]
    onehot = (top_e[:, :, None] == jnp.arange(N_EXPERTS, dtype=jnp.int32)).astype(jnp.int32)
    per_tok = jnp.sum(onehot, axis=1)
    incl = jnp.cumsum(per_tok, axis=0)
    excl = incl - per_tok
    counts = incl[-1]
    padded = (counts + MOE_BM - 1) // MOE_BM * MOE_BM
    pad_end = jnp.cumsum(padded)
    pad_start = pad_end - padded
    pos = jnp.sum(onehot * (excl + pad_start)[:, None, :], axis=-1).reshape(t * TOP_K)
    blk_start = jnp.arange(n_blk, dtype=jnp.int32) * MOE_BM
    blk_expert = jnp.minimum(jnp.sum((pad_end[None, :] <= blk_start[:, None]).astype(jnp.int32), axis=1),
                             N_EXPERTS - 1)

    n_slots = n_blk * MOE_BM
    n_free = n_slots - t * TOP_K
    free_cnt = padded - counts
    free_end = jnp.cumsum(free_cnt)
    j = jnp.arange(n_free, dtype=jnp.int32)
    owner = (free_end[None, :] <= j[:, None]).astype(jnp.int32)
    e_j = jnp.sum(owner, axis=1)
    own = (e_j[:, None] == jnp.arange(N_EXPERTS, dtype=jnp.int32)).astype(jnp.int32)
    in_expert = jnp.sum(own * (pad_start + counts - (free_end - free_cnt))[None, :], axis=1) + j
    free_slot = jnp.where(e_j < N_EXPERTS, in_expert, pad_end[-1] + j - free_end[-1])
    keys = jnp.concatenate([pos, free_slot]).astype(jnp.int32)
    toks = jnp.concatenate([jnp.arange(t * TOP_K, dtype=jnp.int32) // TOP_K, jnp.zeros((n_free,), jnp.int32)])
    _, slot_tok = lax.sort((keys, toks), num_keys=1)
    return pos.astype(jnp.int32), slot_tok.reshape(n_blk, 1, MOE_BM), blk_expert.astype(jnp.int32)


def _row_copy(src_ref, src_row, dst_ref, dst_row, sem):
    return pltpu.make_async_copy(src_ref.at[pl.ds(src_row, 1), :], dst_ref.at[pl.ds(dst_row, 1), :], sem)


def _expert_kernel(be_ref, tok_ref, tok_next_ref, h2_ref, wg_ref, wu_ref, wd_ref, y_ref, xa_ref, xb_ref, sem):
    del be_ref
    b = pl.program_id(0)
    bufs = (xa_ref, xb_ref)

    def gather(t_ref, s):
        for r in range(MOE_BM):
            _row_copy(h2_ref, t_ref[0, r], bufs[s], r, sem.at[s]).start()

    def wait(s):
        for r in range(MOE_BM):
            _row_copy(h2_ref, 0, bufs[s], r, sem.at[s]).wait()

    @pl.when(b == 0)
    def _():
        gather(tok_ref, 0)

    def step(s):
        wait(s)
        x = bufs[s][...].astype(bf16)
        hid = (jax.nn.silu(jnp.dot(x, wg_ref[...], preferred_element_type=f32))
               * jnp.dot(x, wu_ref[...], preferred_element_type=f32))
        y_ref[...] = jnp.dot(hid.astype(bf16), wd_ref[...], preferred_element_type=f32)
        gather(tok_next_ref, 1 - s)

        @pl.when(b == pl.num_programs(0) - 1)
        def _():
            wait(1 - s)

    for s in range(2):
        pl.when(b % 2 == s)(functools.partial(step, s))


def _combine_kernel(pos_ref, pos_next_ref, base_ref, w_ref, ys_ref, out_ref, buf_ref, sem):
    tc = base_ref.shape[0]
    i = pl.program_id(0)
    slot = i % 2

    def gather(p_ref, s):
        def body(r, c):
            for kk in range(TOP_K):
                _row_copy(ys_ref, p_ref[r * TOP_K + kk], buf_ref.at[s, kk], r, sem.at[s]).start()
            return c
        lax.fori_loop(0, tc, body, 0)

    def wait(s):
        def body(r, c):
            for kk in range(TOP_K):
                _row_copy(ys_ref, 0, buf_ref.at[s, kk], r, sem.at[s]).wait()
            return c
        lax.fori_loop(0, tc, body, 0)

    @pl.when(i == 0)
    def _():
        gather(pos_ref, 0)

    gather(pos_next_ref, 1 - slot)
    wait(slot)
    w = w_ref[...]
    acc = base_ref[...]
    for kk in range(TOP_K):
        acc = acc + w[:, kk:kk + 1] * buf_ref[slot, kk]
    out_ref[...] = acc

    @pl.when(i == pl.num_programs(0) - 1)
    def _():
        wait(1 - slot)


def _routed_experts(h2, base, top_e, top_w, weg, weu, wed):
    t = base.shape[0]
    n_assign = t * TOP_K
    n_blk = -(-n_assign // MOE_BM) + N_EXPERTS
    pos, slot_tok, blk_expert = _routing_tables(top_e[:, :TOP_K], n_blk)

    tok_spec = lambda nxt: pl.BlockSpec((None, 1, MOE_BM), lambda b, be: (jnp.minimum(b + nxt, n_blk - 1), 0, 0),
                                        memory_space=pltpu.SMEM)
    ys = pl.pallas_call(
        _expert_kernel,
        out_shape=jax.ShapeDtypeStruct((n_blk * MOE_BM, D_MODEL), f32),
        grid_spec=pltpu.PrefetchScalarGridSpec(
            num_scalar_prefetch=1,
            grid=(n_blk,),
            in_specs=[tok_spec(0), tok_spec(1),
                      pl.BlockSpec(memory_space=pl.ANY),
                      pl.BlockSpec((None, D_MODEL, D_EXPERT), lambda b, be: (be[b], 0, 0)),
                      pl.BlockSpec((None, D_MODEL, D_EXPERT), lambda b, be: (be[b], 0, 0)),
                      pl.BlockSpec((None, D_EXPERT, D_MODEL), lambda b, be: (be[b], 0, 0))],
            out_specs=pl.BlockSpec((MOE_BM, D_MODEL), lambda b, be: (b, 0)),
            scratch_shapes=[pltpu.VMEM((MOE_BM, D_MODEL), f32), pltpu.VMEM((MOE_BM, D_MODEL), f32),
                            pltpu.SemaphoreType.DMA((2,))]),
        compiler_params=_cparams(("arbitrary",)),
        name="moe_experts",
    )(blk_expert, slot_tok, slot_tok, h2, weg, weu, wed)

    tc = 128
    n_tiles = t // tc
    pos_spec = lambda nxt: pl.BlockSpec((tc * TOP_K,), lambda i: (jnp.minimum(i + nxt, n_tiles - 1),),
                                        memory_space=pltpu.SMEM)
    return pl.pallas_call(
        _combine_kernel,
        out_shape=jax.ShapeDtypeStruct((t, D_MODEL), f32),
        grid=(n_tiles,),
        in_specs=[pos_spec(0), pos_spec(1),
                  pl.BlockSpec((tc, D_MODEL), lambda i: (i, 0)),
                  pl.BlockSpec((tc, LANES), lambda i: (i, 0)),
                  pl.BlockSpec(memory_space=pl.ANY)],
        out_specs=pl.BlockSpec((tc, D_MODEL), lambda i: (i, 0)),
        scratch_shapes=[pltpu.VMEM((2, TOP_K, tc, D_MODEL), f32), pltpu.SemaphoreType.DMA((2,))],
        compiler_params=_cparams(("arbitrary",)),
        name="moe_combine",
    )(pos, pos, base, top_w, ys)


def _in_proj_weight(w_in):
    u, ql, kv, kpe, gs, gm = jnp.split(
        w_in, (D_SSM, D_SSM + Q_LORA, D_SSM + Q_LORA + KV_LORA, D_SSM + Q_LORA + KV_LORA + QK_ROPE,
               D_SSM + Q_LORA + KV_LORA + QK_ROPE + D_MODEL), axis=1)
    half = QK_ROPE // 2
    kpe_ext = jnp.concatenate([kpe, kpe[:, half:], kpe[:, :half]], axis=1)
    pad = jnp.zeros((w_in.shape[0], N_PROJ - OFF_KPE - 2 * QK_ROPE), w_in.dtype)
    return jnp.concatenate([gs, gm, u, ql, kv, kpe_ext, pad], axis=1).astype(bf16)


def _layer(x, positions, norm1_g, w_in, q_lat_g, w_q_up, kv_lat_g, w_kv_up, q_norm_g, k_norm_g,
           ssm_a_re, ssm_a_im, ssm_log_dt, ssm_b_re, ssm_b_im, ssm_c_re, ssm_c_im, ssm_d,
           w_ssm_val, w_ssm_gate, w_mla_proj, w_out, norm2_g, w_router, router_bias,
           w_exp_gate, w_exp_up, w_exp_down, w_sh_gate, w_sh_up, w_sh_down):
    batch, seq, d = x.shape
    t = batch * seq
    x2d = x.reshape(t, d)
    tm_big = min(1024, t)
    proj = _norm_proj(x2d, norm1_g.reshape(1, d), _in_proj_weight(w_in), tm_big, 1024)

    w1, q, al_re, al_im = _ssm_weights(ssm_a_re, ssm_a_im, ssm_log_dt, ssm_b_re, ssm_b_im, ssm_c_re, ssm_c_im)
    y_ssm = _s5_mixer(proj[:, OFF_U:OFF_U + D_SSM], batch, w1, q, al_re, al_im, ssm_d)

    o = _mla_mixer(proj, positions, q_lat_g, w_q_up, kv_lat_g, w_kv_up, q_norm_g, k_norm_g, batch, seq)

    x1 = _mix(y_ssm, proj, o, x2d, w_ssm_val.astype(bf16), w_ssm_gate.astype(bf16),
              w_mla_proj.astype(bf16), w_out.astype(bf16), min(256, t))

    h2, base, top_e, top_w = _moe_pre(x1, norm2_g.reshape(1, d), w_router, router_bias.reshape(1, -1),
                                      w_sh_gate.astype(bf16), w_sh_up.astype(bf16), w_sh_down.astype(bf16),
                                      min(256, t))
    out = _routed_experts(h2, base, top_e, top_w, w_exp_gate.astype(bf16), w_exp_up.astype(bf16),
                          w_exp_down.astype(bf16))
    return out.reshape(batch, seq, d)


def kernel(x, positions, norm1_g, w_in, q_lat_g, w_q_up, kv_lat_g, w_kv_up, q_norm_g, k_norm_g, ssm_a_re, ssm_a_im, ssm_log_dt, ssm_b_re, ssm_b_im, ssm_c_re, ssm_c_im, ssm_d, w_ssm_val, w_ssm_gate, w_mla_proj, w_out, norm2_g, w_router, router_bias, w_exp_gate, w_exp_up, w_exp_down, w_sh_gate, w_sh_up, w_sh_down):
    layer_params = (norm1_g, w_in, q_lat_g, w_q_up, kv_lat_g, w_kv_up, q_norm_g, k_norm_g,
                    ssm_a_re, ssm_a_im, ssm_log_dt, ssm_b_re, ssm_b_im, ssm_c_re, ssm_c_im, ssm_d,
                    w_ssm_val, w_ssm_gate, w_mla_proj, w_out, norm2_g, w_router, router_bias,
                    w_exp_gate, w_exp_up, w_exp_down, w_sh_gate, w_sh_up, w_sh_down)
    for layer in range(norm1_g.shape[0]):
        x = _layer(x, positions, *[p[layer] for p in layer_params])
    return x
```

```python
import functools

import jax
import jax.numpy as jnp
from jax import lax
from jax.experimental import pallas as pl
from jax.experimental.pallas import tpu as pltpu

f32 = jnp.float32
bf16 = jnp.bfloat16

D_MODEL = 2048
CHUNK = 64
EPS = 1e-6
D_SSM = 1024
SSM_GROUP_CH = 16
N_SSM_GROUPS = 64
SSM_STATE = 64
LAMBDA_RE_MAX = -1e-4
QK_NOPE = 128
QK_ROPE = 64
QK_DIM = QK_NOPE + QK_ROPE
V_HEAD = 128
MLA_HEADS = 8
Q_LORA = 512
KV_LORA = 256
ROPE_THETA = 10000.0
N_EXPERTS = 64
TOP_K = 8
N_EXPERT_GROUPS = 8
TOPK_GROUPS = 4
D_EXPERT = 512
D_SHARED = 512
ROUTED_SCALE = 2.5

LOG2_E = 1.4426950408889634
LANES = 128
ROW_TILES = D_MODEL // LANES
VMEM_LIMIT = 52 * 1024 * 1024

SSM_L = 8
SSM_SG = 8
SSM_SG_CH = D_SSM // SSM_SG
SSM_ROW = SSM_L * SSM_SG_CH
SSM_HALF = SSM_SG_CH // SSM_GROUP_CH * SSM_STATE

OFF_GS, OFF_GM, OFF_U, OFF_QL, OFF_KV, OFF_KPE = 0, 2048, 4096, 5120, 5632, 5888
N_PROJ = 6144

MOE_BM = 256


def _cparams(sem, vmem=VMEM_LIMIT):
    return pltpu.CompilerParams(dimension_semantics=sem, vmem_limit_bytes=vmem)


def _gelu_tanh(x):
    return 0.5 * x * (1.0 + jnp.tanh(0.7978845608028654 * (x + 0.044715 * (x * x * x))))


def _norm_proj_kernel(x_ref, g_ref, w_ref, o_ref, h_ref):
    @pl.when(pl.program_id(1) == 0)
    def _():
        x = x_ref[...]
        ms = jnp.mean(x * x, axis=-1, keepdims=True)
        h_ref[...] = (x * lax.rsqrt(ms + EPS) * g_ref[...]).astype(bf16)

    o_ref[...] = jnp.dot(h_ref[...], w_ref[...], preferred_element_type=f32).astype(o_ref.dtype)


def _norm_proj(x2d, g, w, tm, tn):
    t, d = x2d.shape
    n = w.shape[1]
    return pl.pallas_call(
        _norm_proj_kernel,
        out_shape=jax.ShapeDtypeStruct((t, n), bf16),
        grid=(t // tm, n // tn),
        in_specs=[pl.BlockSpec((tm, d), lambda i, j: (i, 0)),
                  pl.BlockSpec((1, d), lambda i, j: (0, 0)),
                  pl.BlockSpec((d, tn), lambda i, j: (0, j))],
        out_specs=pl.BlockSpec((tm, tn), lambda i, j: (i, j)),
        scratch_shapes=[pltpu.VMEM((tm, d), bf16)],
        compiler_params=_cparams(("parallel", "arbitrary")),
        name="norm_in_proj",
    )(x2d, g, w)


def _ssm_weights(a_re, a_im, log_dt, b_re, b_im, c_re, c_im):
    hi = lax.Precision.HIGHEST
    g, n = a_re.shape
    c = b_re.shape[2]
    gl = g // SSM_SG
    dt = jnp.exp(log_dt)[:, None]
    lam_re = jnp.minimum(a_re, LAMBDA_RE_MAX)
    lam_im = a_im
    mag = jnp.exp(lam_re * dt)
    ang = lam_im * dt
    lb_re = mag * jnp.cos(ang)
    lb_im = mag * jnp.sin(ang)
    den = lam_re * lam_re + lam_im * lam_im
    num_re = lb_re - 1.0
    coef_re = (num_re * lam_re + lb_im * lam_im) / den
    coef_im = (lb_im * lam_re - num_re * lam_im) / den
    bb_re = coef_re[..., None] * b_re - coef_im[..., None] * b_im
    bb_im = coef_re[..., None] * b_im + coef_im[..., None] * b_re
    pr, pi = [jnp.ones_like(lb_re)], [jnp.zeros_like(lb_re)]
    for _ in range(SSM_L):
        pr_n = pr[-1] * lb_re - pi[-1] * lb_im
        pi_n = pr[-1] * lb_im + pi[-1] * lb_re
        pr.append(pr_n)
        pi.append(pi_n)
    pw_re = jnp.stack(pr)
    pw_im = jnp.stack(pi)
    eye = jnp.eye(gl, dtype=f32)

    xb_re = pw_re[:SSM_L, :, :, None] * bb_re[None] - pw_im[:SSM_L, :, :, None] * bb_im[None]
    xb_im = pw_re[:SSM_L, :, :, None] * bb_im[None] + pw_im[:SSM_L, :, :, None] * bb_re[None]

    k = (jnp.einsum('gcn,lgnd->lgdc', c_re, xb_re, precision=hi)
         - jnp.einsum('gcn,lgnd->lgdc', c_im, xb_im, precision=hi))
    kbd = jnp.einsum('lsgdc,gh->lsgdhc', k.reshape(SSM_L, SSM_SG, gl, c, c), eye)
    kbd = kbd.reshape(SSM_L, SSM_SG, SSM_SG_CH, SSM_SG_CH)
    kbd = jnp.concatenate([kbd, jnp.zeros_like(kbd[:1])], axis=0)
    jj = jnp.arange(SSM_L)[:, None]
    tt = jnp.arange(SSM_L)[None, :]
    lag = jnp.where(tt >= jj, tt - jj, SSM_L)
    m = kbd[lag]
    m = m.transpose(2, 0, 3, 1, 4).reshape(SSM_SG, SSM_ROW, SSM_ROW)

    def state_proj(xb):
        xr = xb[::-1].reshape(SSM_L, SSM_SG, gl, n, c).transpose(1, 0, 2, 4, 3)
        return jnp.einsum('sjgcn,gh->sjgchn', xr, eye)
    p = jnp.stack([state_proj(xb_re), state_proj(xb_im)], axis=4)
    p = p.reshape(SSM_SG, SSM_ROW, 2 * SSM_HALF)

    w_re = pw_re[1:]
    w_im = pw_im[1:]
    q_re = c_re[None] * w_re[:, :, None, :] - c_im[None] * w_im[:, :, None, :]
    q_im = -(c_re[None] * w_im[:, :, None, :] + c_im[None] * w_re[:, :, None, :])

    def out_proj(qp):
        qr = qp.reshape(SSM_L, SSM_SG, gl, c, n).transpose(1, 2, 4, 0, 3)
        return jnp.einsum('sgntc,gh->sgnthc', qr, eye)
    q = jnp.stack([out_proj(q_re), out_proj(q_im)], axis=1)
    q = q.reshape(SSM_SG, 2 * SSM_HALF, SSM_ROW)

    w1 = jnp.concatenate([m, p], axis=2).astype(bf16)
    al_re = pw_re[SSM_L].reshape(SSM_SG, 1, SSM_HALF)
    al_im = pw_im[SSM_L].reshape(SSM_SG, 1, SSM_HALF)
    return w1, q.astype(bf16), al_re, al_im


def _ssm_a_kernel(u_ref, w_ref, yi_ref, sl_ref):
    r = jnp.dot(u_ref[...], w_ref[...], preferred_element_type=f32)
    yi_ref[...] = r[:, :SSM_ROW]
    sl_ref[...] = r[:, SSM_ROW:]


def _ssm_scan_kernel(sl_ref, ar_ref, ai_ref, sp_ref, st_ref):
    nb, tr, _ = sl_ref.shape

    @pl.when(pl.program_id(1) == 0)
    def _():
        st_ref[...] = jnp.zeros(st_ref.shape, f32)

    ar = ar_ref[...]
    ai = ai_ref[...]

    def body(k, carry):
        new = []
        for b in range(nb):
            sr, si = carry[b]
            sp_ref[b, pl.ds(k, 1), :] = jnp.concatenate([sr, si], axis=1)
            row = sl_ref[b, pl.ds(k, 1), :]
            br = row[:, :SSM_HALF]
            bi = row[:, SSM_HALF:]
            new.append((ar * sr - ai * si + br, ar * si + ai * sr + bi))
        return tuple(new)

    init = tuple((st_ref[b][:, :SSM_HALF], st_ref[b][:, SSM_HALF:]) for b in range(nb))
    fin = lax.fori_loop(0, tr, body, init)
    for b in range(nb):
        st_ref[b] = jnp.concatenate(fin[b], axis=1)


def _ssm_b_kernel(yi_ref, sp_ref, u_ref, q_ref, d_ref, y_ref):
    y = (yi_ref[...]
         + jnp.dot(sp_ref[...].astype(bf16), q_ref[...], preferred_element_type=f32)
         + d_ref[...] * u_ref[...].astype(f32))
    y_ref[...] = _gelu_tanh(y).astype(y_ref.dtype)


def _s5_mixer(u2d, batch, w1, q, al_re, al_im, d):
    t = u2d.shape[0]
    r = t // SSM_L
    rb = r // batch
    tr = min(512, rb)
    u_sg = u2d.reshape(r, SSM_L, SSM_SG, SSM_SG_CH).transpose(2, 0, 1, 3).reshape(SSM_SG, r, SSM_ROW)
    yi, sl = pl.pallas_call(
        _ssm_a_kernel,
        out_shape=(jax.ShapeDtypeStruct((SSM_SG, r, SSM_ROW), f32),
                   jax.ShapeDtypeStruct((SSM_SG, r, 2 * SSM_HALF), f32)),
        grid=(SSM_SG, r // tr),
        in_specs=[pl.BlockSpec((None, tr, SSM_ROW), lambda s, i: (s, i, 0)),
                  pl.BlockSpec((None, SSM_ROW, SSM_ROW + 2 * SSM_HALF), lambda s, i: (s, 0, 0))],
        out_specs=(pl.BlockSpec((None, tr, SSM_ROW), lambda s, i: (s, i, 0)),
                   pl.BlockSpec((None, tr, 2 * SSM_HALF), lambda s, i: (s, i, 0))),
        compiler_params=_cparams(("parallel", "parallel")),
        name="ssm_intra",
    )(u_sg, w1)

    sl4 = sl.reshape(SSM_SG, batch, rb, 2 * SSM_HALF)
    sp4 = pl.pallas_call(
        _ssm_scan_kernel,
        out_shape=jax.ShapeDtypeStruct(sl4.shape, f32),
        grid=(SSM_SG, rb // tr),
        in_specs=[pl.BlockSpec((None, batch, tr, 2 * SSM_HALF), lambda s, i: (s, 0, i, 0)),
                  pl.BlockSpec((None, 1, SSM_HALF), lambda s, i: (s, 0, 0)),
                  pl.BlockSpec((None, 1, SSM_HALF), lambda s, i: (s, 0, 0))],
        out_specs=pl.BlockSpec((None, batch, tr, 2 * SSM_HALF), lambda s, i: (s, 0, i, 0)),
        scratch_shapes=[pltpu.VMEM((batch, 1, 2 * SSM_HALF), f32)],
        compiler_params=_cparams(("parallel", "arbitrary")),
        name="ssm_scan",
    )(sl4, al_re, al_im)
    sp = sp4.reshape(SSM_SG, r, 2 * SSM_HALF)

    d_sg = jnp.tile(d.reshape(SSM_SG, 1, SSM_SG_CH), (1, 1, SSM_L))
    y_sg = pl.pallas_call(
        _ssm_b_kernel,
        out_shape=jax.ShapeDtypeStruct((SSM_SG, r, SSM_ROW), bf16),
        grid=(SSM_SG, r // tr),
        in_specs=[pl.BlockSpec((None, tr, SSM_ROW), lambda s, i: (s, i, 0)),
                  pl.BlockSpec((None, tr, 2 * SSM_HALF), lambda s, i: (s, i, 0)),
                  pl.BlockSpec((None, tr, SSM_ROW), lambda s, i: (s, i, 0)),
                  pl.BlockSpec((None, 2 * SSM_HALF, SSM_ROW), lambda s, i: (s, 0, 0)),
                  pl.BlockSpec((None, 1, SSM_ROW), lambda s, i: (s, 0, 0))],
        out_specs=pl.BlockSpec((None, tr, SSM_ROW), lambda s, i: (s, i, 0)),
        compiler_params=_cparams(("parallel", "parallel")),
        name="ssm_inter",
    )(yi, sp, u_sg, q, d_sg)
    return y_sg.reshape(SSM_SG, r, SSM_L, SSM_SG_CH).transpose(1, 2, 0, 3).reshape(t, D_SSM)


def _mla_prep_kernel(ql_ref, kvp_ref, qg_ref, kvg_ref, wq_ref, wkv_ref, cq_ref, sq_ref, ck_ref, sk_ref,
                     gqn_ref, gkn_ref, q_ref, kt_ref, v_ref):
    tm = ql_ref.shape[0]
    lane = lax.broadcasted_iota(jnp.int32, (tm, LANES), 1)
    rope_lanes = lane < QK_ROPE

    ql = ql_ref[...].astype(f32)
    hq = (ql * lax.rsqrt(jnp.mean(ql * ql, axis=-1, keepdims=True) + EPS) * qg_ref[...]).astype(bf16)
    kvp = kvp_ref[...].astype(f32)
    kv = kvp[:, :KV_LORA]
    hkv = (kv * lax.rsqrt(jnp.mean(kv * kv, axis=-1, keepdims=True) + EPS) * kvg_ref[...]).astype(bf16)
    kx = kvp[:, KV_LORA:KV_LORA + LANES]
    kss = jnp.sum(jnp.where(rope_lanes, kx * kx, 0.0), axis=-1, keepdims=True)
    ro = kx * ck_ref[...] + pltpu.roll(kx, QK_ROPE, 1) * sk_ref[...]

    for h in range(MLA_HEADS):
        qx = jnp.dot(hq, wq_ref[h], preferred_element_type=f32)
        x1 = qx[:, :QK_NOPE]
        x2 = qx[:, QK_NOPE:]
        ssq = (jnp.sum(x1 * x1, axis=-1, keepdims=True)
               + jnp.sum(jnp.where(rope_lanes, x2 * x2, 0.0), axis=-1, keepdims=True))
        rinv = lax.rsqrt(ssq * (1.0 / QK_DIM) + EPS) * (QK_DIM ** -0.5 * LOG2_E)
        q_ref[h, :, :QK_NOPE] = (x1 * gqn_ref[...] * rinv).astype(q_ref.dtype)
        q_ref[h, :, QK_NOPE:] = ((x2 * cq_ref[...] + pltpu.roll(x2, QK_ROPE, 1) * sq_ref[...])
                                 * rinv).astype(q_ref.dtype)

        kvx = jnp.dot(hkv, wkv_ref[h], preferred_element_type=f32)
        kn = kvx[:, :QK_NOPE]
        rk = lax.rsqrt((jnp.sum(kn * kn, axis=-1, keepdims=True) + kss) * (1.0 / QK_DIM) + EPS)
        k = jnp.concatenate([kn * gkn_ref[...] * rk, ro * rk], axis=1)
        kt_ref[h] = k.T.astype(kt_ref.dtype)
        v_ref[h] = kvx[:, QK_NOPE:].astype(v_ref.dtype)


def _attn_kernel(q_ref, kt_ref, v_ref, o_ref, s_ref, m_ref, l_ref, acc_ref, *, tq, sub):
    qi = pl.program_id(2)
    m_ref[...] = jnp.full(m_ref.shape, -jnp.inf, f32)
    l_ref[...] = jnp.zeros(l_ref.shape, f32)
    acc_ref[...] = jnp.zeros(acc_ref.shape, f32)

    def scores(j):
        start = pl.multiple_of(j * tq, tq)
        return jnp.dot(q_ref[...], kt_ref[:, pl.ds(start, tq)], preferred_element_type=f32)

    def update(i, s, v):
        rows = pl.ds(i * sub, sub)
        m_prev = m_ref[rows, :]
        m_new = jnp.maximum(m_prev, jnp.max(s, axis=-1, keepdims=True))
        alpha = jnp.exp2(m_prev - m_new)
        p = jnp.exp2(s - m_new)
        l_ref[rows, :] = alpha * l_ref[rows, :] + jnp.sum(p, axis=-1, keepdims=True)
        acc_ref[rows, :] = alpha * acc_ref[rows, :] + jnp.dot(p.astype(v.dtype), v, preferred_element_type=f32)
        m_ref[rows, :] = m_new

    s_ref[...] = scores(0)

    def full_body(j, c):
        s_next = scores(j + 1)
        v = v_ref[pl.ds(pl.multiple_of(j * tq, tq), tq), :]
        for i in range(tq // sub):
            update(i, s_ref[pl.ds(i * sub, sub), :], v)
        s_ref[...] = s_next
        return c

    lax.fori_loop(0, qi, full_body, 0)

    start = pl.multiple_of(qi * tq, tq)
    for i in range(tq // sub):
        nk = (i + 1) * sub
        s = s_ref[pl.ds(i * sub, sub), pl.ds(0, nk)]
        q_chunk = (i * sub + lax.broadcasted_iota(jnp.int32, (sub, nk), 0)) // CHUNK
        k_chunk = lax.broadcasted_iota(jnp.int32, (sub, nk), 1) // CHUNK
        update(i, jnp.where(k_chunk <= q_chunk, s, -jnp.inf), v_ref[pl.ds(start, nk), :])
    o_ref[...] = (acc_ref[...] / l_ref[...]).astype(o_ref.dtype)


def _mla_mixer(proj, positions, q_lat_g, w_q_up, kv_lat_g, w_kv_up, q_norm_g, k_norm_g, batch, seq):
    t = proj.shape[0]
    half = QK_ROPE // 2
    inv = ROPE_THETA ** (-jnp.arange(half, dtype=f32) * (2.0 / QK_ROPE))
    ang = positions.reshape(t).astype(f32)[:, None] * inv
    cos, sin = jnp.cos(ang), jnp.sin(ang)
    zero = jnp.zeros((t, 2 * half), f32)

    def rope_tables(gain):
        g1, g2 = gain[QK_NOPE:QK_NOPE + half], gain[QK_NOPE + half:]
        return (jnp.concatenate([g1 * cos, g2 * cos, zero], axis=1),
                jnp.concatenate([-g2 * sin, g1 * sin, zero], axis=1))

    cq, sq = rope_tables(q_norm_g)
    ck, sk = rope_tables(k_norm_g)
    wq = w_q_up.reshape(Q_LORA, MLA_HEADS, QK_DIM).transpose(1, 0, 2)
    r1, r2 = wq[..., QK_NOPE:QK_NOPE + half], wq[..., QK_NOPE + half:]
    wq = jnp.concatenate([wq, r2, r1], axis=-1).astype(bf16)
    wkv = w_kv_up.reshape(KV_LORA, MLA_HEADS, QK_NOPE + V_HEAD).transpose(1, 0, 2).astype(bf16)

    tm = min(512, seq)
    nsb = seq // tm
    row = lambda i: (i, 0)
    const = lambda i: (0, 0)
    const3 = lambda i: (0, 0, 0)
    head_o = lambda i: (i // nsb, 0, i % nsb, 0)
    q, kt, v = pl.pallas_call(
        _mla_prep_kernel,
        out_shape=(jax.ShapeDtypeStruct((batch, MLA_HEADS, seq, 2 * LANES), bf16),
                   jax.ShapeDtypeStruct((batch, MLA_HEADS, 2 * LANES, seq), bf16),
                   jax.ShapeDtypeStruct((batch, MLA_HEADS, seq, V_HEAD), bf16)),
        grid=(t // tm,),
        in_specs=[pl.BlockSpec((tm, Q_LORA), lambda i: (i, OFF_QL // Q_LORA)),
                  pl.BlockSpec((tm, 512), lambda i: (i, OFF_KV // 512)),
                  pl.BlockSpec((1, Q_LORA), const),
                  pl.BlockSpec((1, KV_LORA), const),
                  pl.BlockSpec(wq.shape, const3),
                  pl.BlockSpec(wkv.shape, const3),
                  pl.BlockSpec((tm, LANES), row), pl.BlockSpec((tm, LANES), row),
                  pl.BlockSpec((tm, LANES), row), pl.BlockSpec((tm, LANES), row),
                  pl.BlockSpec((1, QK_NOPE), const), pl.BlockSpec((1, QK_NOPE), const)],
        out_specs=(pl.BlockSpec((None, MLA_HEADS, tm, 2 * LANES), head_o),
                   pl.BlockSpec((None, MLA_HEADS, 2 * LANES, tm), lambda i: (i // nsb, 0, 0, i % nsb)),
                   pl.BlockSpec((None, MLA_HEADS, tm, V_HEAD), head_o)),
        compiler_params=_cparams(("parallel",)),
        name="mla_prep",
    )(proj, proj, q_lat_g.reshape(1, -1), kv_lat_g.reshape(1, -1), wq, wkv, cq, sq, ck, sk,
      q_norm_g[:QK_NOPE].reshape(1, -1), k_norm_g[:QK_NOPE].reshape(1, -1))

    tq = min(512, seq)
    o = pl.pallas_call(
        functools.partial(_attn_kernel, tq=tq, sub=min(512, tq)),
        out_shape=jax.ShapeDtypeStruct((batch, seq, MLA_HEADS * V_HEAD), bf16),
        grid=(batch, MLA_HEADS, seq // tq),
        in_specs=[pl.BlockSpec((None, None, tq, 2 * LANES), lambda b, h, i: (b, h, i, 0)),
                  pl.BlockSpec((None, None, 2 * LANES, seq), lambda b, h, i: (b, h, 0, 0)),
                  pl.BlockSpec((None, None, seq, V_HEAD), lambda b, h, i: (b, h, 0, 0))],
        out_specs=pl.BlockSpec((None, tq, V_HEAD), lambda b, h, i: (b, i, h)),
        scratch_shapes=[pltpu.VMEM((tq, tq), f32), pltpu.VMEM((tq, 1), f32), pltpu.VMEM((tq, 1), f32),
                        pltpu.VMEM((tq, V_HEAD), f32)],
        compiler_params=_cparams(("parallel", "parallel", "arbitrary")),
        name="mla_attention",
    )(q, kt, v)
    return o.reshape(t, MLA_HEADS * V_HEAD)


def _mix_kernel(y_ref, gs_ref, o_ref, gm_ref, x_ref, wv_ref, wg_ref, wp_ref, wo_ref, out_ref):
    y = y_ref[...]
    val = jnp.dot(y, wv_ref[...], preferred_element_type=f32)
    gate = jnp.dot(y, wg_ref[...], preferred_element_type=f32)
    ssm = val * jax.nn.sigmoid(gate) * jax.nn.sigmoid(gs_ref[...].astype(f32))
    mla = jnp.dot(o_ref[...], wp_ref[...], preferred_element_type=f32) * jax.nn.sigmoid(gm_ref[...].astype(f32))
    merged = (ssm + mla).astype(bf16)
    out_ref[...] = x_ref[...] + jnp.dot(merged, wo_ref[...], preferred_element_type=f32)


def _mix(y, proj, o, x2d, wv, wg, wp, wo, tm):
    t = x2d.shape[0]
    row = lambda i: (i, 0)
    const = lambda i: (0, 0)
    resident = lambda shape: pl.BlockSpec(shape, const, pipeline_mode=pl.Buffered(1))
    return pl.pallas_call(
        _mix_kernel,
        out_shape=jax.ShapeDtypeStruct((t, D_MODEL), f32),
        grid=(t // tm,),
        in_specs=[pl.BlockSpec((tm, D_SSM), row),
                  pl.BlockSpec((tm, D_MODEL), lambda i: (i, OFF_GS // D_MODEL)),
                  pl.BlockSpec((tm, MLA_HEADS * V_HEAD), row),
                  pl.BlockSpec((tm, D_MODEL), lambda i: (i, OFF_GM // D_MODEL)),
                  pl.BlockSpec((tm, D_MODEL), row),
                  resident(wv.shape), resident(wg.shape), resident(wp.shape), resident(wo.shape)],
        out_specs=pl.BlockSpec((tm, D_MODEL), row),
        compiler_params=_cparams(("parallel",)),
        name="merge_out_proj",
    )(y, proj, o, proj, x2d, wv, wg, wp, wo)


def _moe_pre_kernel(x_ref, g_ref, wr_ref, rb_ref, wsg_ref, wsu_ref, wsd_ref, h2_ref, base_ref, te_ref, tw_ref):
    x = x_ref[...]
    tm = x.shape[0]
    h2 = x * lax.rsqrt(jnp.mean(x * x, axis=-1, keepdims=True) + EPS) * g_ref[...]
    for j in range(ROW_TILES):
        h2_ref[pl.ds(j, tm, stride=ROW_TILES), :] = h2[:, j * LANES:(j + 1) * LANES]
    hb = h2.astype(bf16)
    hid = (jax.nn.silu(jnp.dot(hb, wsg_ref[...], preferred_element_type=f32))
           * jnp.dot(hb, wsu_ref[...], preferred_element_type=f32))
    base_ref[...] = x + jnp.dot(hid.astype(bf16), wsd_ref[...], preferred_element_type=f32)

    logits = jnp.dot(h2, wr_ref[...], preferred_element_type=f32, precision=lax.Precision.HIGHEST)
    scores = jax.nn.sigmoid(logits)
    sel = scores + rb_ref[...]
    neg = -jnp.inf
    per_grp = N_EXPERTS // N_EXPERT_GROUPS
    lane = lax.broadcasted_iota(jnp.int32, (tm, N_EXPERTS), 1).astype(f32)
    grp = jnp.floor(lane * (1.0 / per_grp))

    def first_max(vals, ids, sentinel):
        mx = jnp.max(vals, axis=-1, keepdims=True)
        return mx, jnp.min(jnp.where(vals == mx, ids, sentinel), axis=-1, keepdims=True)

    gscore = jnp.zeros((tm, N_EXPERTS), f32)
    for g in range(N_EXPERT_GROUPS):
        in_g = grp == float(g)
        mg = jnp.where(in_g, sel, neg)
        m1, i1 = first_max(mg, lane, float(N_EXPERTS))
        m2 = jnp.max(jnp.where(lane == i1, neg, mg), axis=-1, keepdims=True)
        gscore = jnp.where(in_g, m1 + m2, gscore)
    allowed = jnp.zeros((tm, N_EXPERTS), f32)
    for _ in range(TOPK_GROUPS):
        _, gi = first_max(gscore, grp, float(N_EXPERT_GROUPS))
        hit = grp == gi
        allowed = jnp.where(hit, 1.0, allowed)
        gscore = jnp.where(hit, neg, gscore)
    masked = jnp.where(allowed > 0.0, sel, neg)
    out_lane = lax.broadcasted_iota(jnp.int32, (tm, LANES), 1)
    te = jnp.zeros((tm, LANES), f32)
    tw = jnp.zeros((tm, LANES), f32)
    wsum = jnp.zeros((tm, 1), f32)
    for kk in range(TOP_K):
        _, idx = first_max(masked, lane, float(N_EXPERTS))
        hit = lane == idx
        wk = jnp.sum(jnp.where(hit, scores, 0.0), axis=-1, keepdims=True)
        masked = jnp.where(hit, neg, masked)
        te = jnp.where(out_lane == kk, idx, te)
        tw = jnp.where(out_lane == kk, wk, tw)
        wsum = wsum + wk
    te_ref[...] = te.astype(jnp.int32)
    tw_ref[...] = tw / wsum * ROUTED_SCALE


def _moe_pre(x1, g2, w_router, router_bias, wsg, wsu, wsd, tm):
    t = x1.shape[0]
    row = lambda i: (i, 0)
    const = lambda i: (0, 0)
    return pl.pallas_call(
        _moe_pre_kernel,
        out_shape=(jax.ShapeDtypeStruct((t * ROW_TILES, LANES), f32),
                   jax.ShapeDtypeStruct((t, D_MODEL), f32),
                   jax.ShapeDtypeStruct((t, LANES), jnp.int32),
                   jax.ShapeDtypeStruct((t, LANES), f32)),
        grid=(t // tm,),
        in_specs=[pl.BlockSpec((tm, D_MODEL), row),
                  pl.BlockSpec((1, D_MODEL), const),
                  pl.BlockSpec(w_router.shape, const),
                  pl.BlockSpec((1, N_EXPERTS), const),
                  pl.BlockSpec(wsg.shape, const), pl.BlockSpec(wsu.shape, const), pl.BlockSpec(wsd.shape, const)],
        out_specs=(pl.BlockSpec((tm * ROW_TILES, LANES), row),
                   pl.BlockSpec((tm, D_MODEL), row),
                   pl.BlockSpec((tm, LANES), row),
                   pl.BlockSpec((tm, LANES), row)),
        compiler_params=_cparams(("parallel",)),
        name="norm2_router_shared",
    )(x1, g2, w_router, router_bias, wsg, wsu, wsd)


def _routing_tables(top_e, n_blk):
    t = top_e.shape[0]
    onehot = (top_e[:, :, None] == jnp.arange(N_EXPERTS, dtype=jnp.int32)).astype(jnp.int32)
    per_tok = jnp.sum(onehot, axis=1)
    incl = jnp.cumsum(per_tok, axis=0)
    excl = incl - per_tok
    counts = incl[-1]
    padded = (counts + MOE_BM - 1) // MOE_BM * MOE_BM
    pad_end = jnp.cumsum(padded)
    pad_start = pad_end - padded
    pos = jnp.sum(onehot * (excl + pad_start)[:, None, :], axis=-1).reshape(t * TOP_K)
    blk_start = jnp.arange(n_blk, dtype=jnp.int32) * MOE_BM
    blk_expert = jnp.minimum(jnp.sum((pad_end[None, :] <= blk_start[:, None]).astype(jnp.int32), axis=1),
                             N_EXPERTS - 1)

    n_slots = n_blk * MOE_BM
    n_free = n_slots - t * TOP_K
    free_cnt = padded - counts
    free_end = jnp.cumsum(free_cnt)
    j = jnp.arange(n_free, dtype=jnp.int32)
    owner = (free_end[None, :] <= j[:, None]).astype(jnp.int32)
    e_j = jnp.sum(owner, axis=1)
    own = (e_j[:, None] == jnp.arange(N_EXPERTS, dtype=jnp.int32)).astype(jnp.int32)
    in_expert = jnp.sum(own * (pad_start + counts - (free_end - free_cnt))[None, :], axis=1) + j
    free_slot = jnp.where(e_j < N_EXPERTS, in_expert, pad_end[-1] + j - free_end[-1])
    keys = jnp.concatenate([pos, free_slot]).astype(jnp.int32)
    toks = jnp.concatenate([jnp.arange(t * TOP_K, dtype=jnp.int32) // TOP_K, jnp.zeros((n_free,), jnp.int32)])
    _, slot_tok = lax.sort((keys, toks), num_keys=1)
    return pos.astype(jnp.int32), slot_tok.reshape(n_blk, 1, MOE_BM), blk_expert.astype(jnp.int32)


def _slab_copy(src_ref, src_row, dst_ref, dst_row, sem):
    def rows(r):
        start = r * ROW_TILES
        return pl.ds(start if isinstance(r, int) else pl.multiple_of(start, ROW_TILES), ROW_TILES)
    return pltpu.make_async_copy(src_ref.at[rows(src_row), :], dst_ref.at[rows(dst_row), :], sem)


def _slab_matrix(ref, n, lead=()):
    return jnp.concatenate([ref[lead + (pl.ds(j, n, stride=ROW_TILES), slice(None))] for j in range(ROW_TILES)],
                           axis=1)


def _expert_kernel(be_ref, tok0_ref, tok1_ref, tok2_ref, h2_ref, wga_ref, wua_ref, wda_ref, wgb_ref, wub_ref,
                   wdb_ref, y_ref, xa_ref, xb_ref, sem):
    del be_ref
    i = pl.program_id(0)

    def gather(t_ref, buf, s):
        def body(g, c):
            for u in range(8):
                r = g * 8 + u
                _slab_copy(h2_ref, t_ref[0, r], buf, r, sem.at[s]).start()
            return c
        lax.fori_loop(0, MOE_BM // 8, body, 0)

    def wait(buf, s):
        for _ in range(MOE_BM):
            _slab_copy(h2_ref, 0, buf, 0, sem.at[s]).wait()

    def experts(buf, wg_ref, wu_ref, wd_ref, half):
        x = _slab_matrix(buf, MOE_BM).astype(bf16)
        hid = (jax.nn.silu(jnp.dot(x, wg_ref[...], preferred_element_type=f32))
               * jnp.dot(x, wu_ref[...], preferred_element_type=f32))
        y_ref[half * MOE_BM:(half + 1) * MOE_BM, :] = jnp.dot(hid.astype(bf16), wd_ref[...],
                                                             preferred_element_type=f32)

    @pl.when(i == 0)
    def _():
        gather(tok0_ref, xa_ref, 0)

    wait(xa_ref, 0)
    gather(tok1_ref, xb_ref, 1)
    experts(xa_ref, wga_ref, wua_ref, wda_ref, 0)
    wait(xb_ref, 1)
    gather(tok2_ref, xa_ref, 0)
    experts(xb_ref, wgb_ref, wub_ref, wdb_ref, 1)

    @pl.when(i == pl.num_programs(0) - 1)
    def _():
        wait(xa_ref, 0)


def _combine_kernel(pos_ref, pos_next_ref, base_ref, w_ref, ys_ref, out_ref, buf_ref, sem):
    tc = base_ref.shape[0]
    i = pl.program_id(0)
    slot = i % 2

    def row_copy(src_row, kk, dst_row, s):
        return pltpu.make_async_copy(ys_ref.at[pl.ds(src_row, 1), :], buf_ref.at[s, kk, pl.ds(dst_row, 1), :],
                                     sem.at[s])

    def gather(p_ref, s):
        def body(r, c):
            for kk in range(TOP_K):
                row_copy(p_ref[r * TOP_K + kk], kk, r, s).start()
            return c
        lax.fori_loop(0, tc, body, 0)

    def wait(s):
        def body(r, c):
            for _ in range(TOP_K):
                row_copy(0, 0, 0, s).wait()
            return c
        lax.fori_loop(0, tc, body, 0)

    @pl.when(i == 0)
    def _():
        gather(pos_ref, 0)

    gather(pos_next_ref, 1 - slot)
    wait(slot)
    w = w_ref[...]
    acc = base_ref[...]
    for kk in range(TOP_K):
        acc = acc + w[:, kk:kk + 1] * buf_ref[slot, kk]
    out_ref[...] = acc

    @pl.when(i == pl.num_programs(0) - 1)
    def _():
        wait(1 - slot)


def _routed_experts(h2, base, top_e, top_w, weg, weu, wed):
    t = base.shape[0]
    n_assign = t * TOP_K
    n_blk = -(-n_assign // MOE_BM) + N_EXPERTS
    n_blk += n_blk % 2
    pos, slot_tok, blk_expert = _routing_tables(top_e[:, :TOP_K], n_blk)

    def tok_spec(blk):
        return pl.BlockSpec((None, 1, MOE_BM), lambda i, be: (blk(i), 0, 0), memory_space=pltpu.SMEM)

    def w_spec(shape, half):
        return pl.BlockSpec((None,) + shape, lambda i, be: (be[2 * i + half], 0, 0))

    slab_rows = MOE_BM * ROW_TILES
    ys = pl.pallas_call(
        _expert_kernel,
        out_shape=jax.ShapeDtypeStruct((n_blk * MOE_BM, D_MODEL), f32),
        grid_spec=pltpu.PrefetchScalarGridSpec(
            num_scalar_prefetch=1,
            grid=(n_blk // 2,),
            in_specs=[tok_spec(lambda i: 0), tok_spec(lambda i: 2 * i + 1),
                      tok_spec(lambda i: jnp.minimum(2 * i + 2, n_blk - 1)),
                      pl.BlockSpec(memory_space=pl.ANY),
                      w_spec((D_MODEL, D_EXPERT), 0), w_spec((D_MODEL, D_EXPERT), 0), w_spec((D_EXPERT, D_MODEL), 0),
                      w_spec((D_MODEL, D_EXPERT), 1), w_spec((D_MODEL, D_EXPERT), 1), w_spec((D_EXPERT, D_MODEL), 1)],
            out_specs=pl.BlockSpec((2 * MOE_BM, D_MODEL), lambda i, be: (i, 0)),
            scratch_shapes=[pltpu.VMEM((slab_rows, LANES), f32), pltpu.VMEM((slab_rows, LANES), f32),
                            pltpu.SemaphoreType.DMA((2,))]),
        compiler_params=_cparams(("arbitrary",)),
        name="moe_experts",
    )(blk_expert, slot_tok, slot_tok, slot_tok, h2, weg, weu, wed, weg, weu, wed)

    tc = 128
    n_tiles = t // tc
    pos_spec = lambda nxt: pl.BlockSpec((tc * TOP_K,), lambda i: (jnp.minimum(i + nxt, n_tiles - 1),),
                                        memory_space=pltpu.SMEM)
    return pl.pallas_call(
        _combine_kernel,
        out_shape=jax.ShapeDtypeStruct((t, D_MODEL), f32),
        grid=(n_tiles,),
        in_specs=[pos_spec(0), pos_spec(1),
                  pl.BlockSpec((tc, D_MODEL), lambda i: (i, 0)),
                  pl.BlockSpec((tc, LANES), lambda i: (i, 0)),
                  pl.BlockSpec(memory_space=pl.ANY)],
        out_specs=pl.BlockSpec((tc, D_MODEL), lambda i: (i, 0)),
        scratch_shapes=[pltpu.VMEM((2, TOP_K, tc, D_MODEL), f32), pltpu.SemaphoreType.DMA((2,))],
        compiler_params=_cparams(("arbitrary",)),
        name="moe_combine",
    )(pos, pos, base, top_w, ys)


def _in_proj_weight(w_in):
    u, ql, kv, kpe, gs, gm = jnp.split(
        w_in, (D_SSM, D_SSM + Q_LORA, D_SSM + Q_LORA + KV_LORA, D_SSM + Q_LORA + KV_LORA + QK_ROPE,
               D_SSM + Q_LORA + KV_LORA + QK_ROPE + D_MODEL), axis=1)
    half = QK_ROPE // 2
    kpe_ext = jnp.concatenate([kpe, kpe[:, half:], kpe[:, :half]], axis=1)
    pad = jnp.zeros((w_in.shape[0], N_PROJ - OFF_KPE - 2 * QK_ROPE), w_in.dtype)
    return jnp.concatenate([gs, gm, u, ql, kv, kpe_ext, pad], axis=1).astype(bf16)


def _layer(x, positions, norm1_g, w_in, q_lat_g, w_q_up, kv_lat_g, w_kv_up, q_norm_g, k_norm_g,
           ssm_a_re, ssm_a_im, ssm_log_dt, ssm_b_re, ssm_b_im, ssm_c_re, ssm_c_im, ssm_d,
           w_ssm_val, w_ssm_gate, w_mla_proj, w_out, norm2_g, w_router, router_bias,
           w_exp_gate, w_exp_up, w_exp_down, w_sh_gate, w_sh_up, w_sh_down):
    batch, seq, d = x.shape
    t = batch * seq
    x2d = x.reshape(t, d)
    tm_big = min(1024, t)
    proj = _norm_proj(x2d, norm1_g.reshape(1, d), _in_proj_weight(w_in), tm_big, 1024)

    w1, q, al_re, al_im = _ssm_weights(ssm_a_re, ssm_a_im, ssm_log_dt, ssm_b_re, ssm_b_im, ssm_c_re, ssm_c_im)
    y_ssm = _s5_mixer(proj[:, OFF_U:OFF_U + D_SSM], batch, w1, q, al_re, al_im, ssm_d)

    o = _mla_mixer(proj, positions, q_lat_g, w_q_up, kv_lat_g, w_kv_up, q_norm_g, k_norm_g, batch, seq)

    x1 = _mix(y_ssm, proj, o, x2d, w_ssm_val.astype(bf16), w_ssm_gate.astype(bf16),
              w_mla_proj.astype(bf16), w_out.astype(bf16), min(256, t))

    h2, base, top_e, top_w = _moe_pre(x1, norm2_g.reshape(1, d), w_router, router_bias.reshape(1, -1),
                                      w_sh_gate.astype(bf16), w_sh_up.astype(bf16), w_sh_down.astype(bf16),
                                      min(256, t))
    out = _routed_experts(h2, base, top_e, top_w, w_exp_gate.astype(bf16), w_exp_up.astype(bf16),
                          w_exp_down.astype(bf16))
    return out.reshape(batch, seq, d)


def kernel(x, positions, norm1_g, w_in, q_lat_g, w_q_up, kv_lat_g, w_kv_up, q_norm_g, k_norm_g, ssm_a_re, ssm_a_im, ssm_log_dt, ssm_b_re, ssm_b_im, ssm_c_re, ssm_c_im, ssm_d, w_ssm_val, w_ssm_gate, w_mla_proj, w_out, norm2_g, w_router, router_bias, w_exp_gate, w_exp_up, w_exp_down, w_sh_gate, w_sh_up, w_sh_down):
    layer_params = (norm1_g, w_in, q_lat_g, w_q_up, kv_lat_g, w_kv_up, q_norm_g, k_norm_g,
                    ssm_a_re, ssm_a_im, ssm_log_dt, ssm_b_re, ssm_b_im, ssm_c_re, ssm_c_im, ssm_d,
                    w_ssm_val, w_ssm_gate, w_mla_proj, w_out, norm2_g, w_router, router_bias,
                    w_exp_gate, w_exp_up, w_exp_down, w_sh_gate, w_sh_up, w_sh_down)
    for layer in range(norm1_g.shape[0]):
        x = _layer(x, positions, *[p[layer] for p in layer_params])
    return x
```

```python
import functools

import jax
import jax.numpy as jnp
from jax import lax
from jax.experimental import pallas as pl
from jax.experimental.pallas import tpu as pltpu

f32 = jnp.float32
bf16 = jnp.bfloat16

D_MODEL = 2048
CHUNK = 64
EPS = 1e-6
D_SSM = 1024
SSM_GROUP_CH = 16
N_SSM_GROUPS = 64
SSM_STATE = 64
LAMBDA_RE_MAX = -1e-4
QK_NOPE = 128
QK_ROPE = 64
QK_DIM = QK_NOPE + QK_ROPE
V_HEAD = 128
MLA_HEADS = 8
Q_LORA = 512
KV_LORA = 256
ROPE_THETA = 10000.0
N_EXPERTS = 64
TOP_K = 8
N_EXPERT_GROUPS = 8
TOPK_GROUPS = 4
D_EXPERT = 512
D_SHARED = 512
ROUTED_SCALE = 2.5

LOG2_E = 1.4426950408889634
LANES = 128
ROW_TILES = D_MODEL // LANES
VMEM_LIMIT = 52 * 1024 * 1024

SSM_L = 8
SSM_SG = 8
SSM_SG_CH = D_SSM // SSM_SG
SSM_ROW = SSM_L * SSM_SG_CH
SSM_HALF = SSM_SG_CH // SSM_GROUP_CH * SSM_STATE

OFF_GS, OFF_GM, OFF_U, OFF_QL, OFF_KV, OFF_KPE = 0, 2048, 4096, 5120, 5632, 5888
N_PROJ = 6144

MOE_BM = 256


def _cparams(sem, vmem=VMEM_LIMIT):
    return pltpu.CompilerParams(dimension_semantics=sem, vmem_limit_bytes=vmem)


def _gelu_tanh(x):
    return 0.5 * x * (1.0 + jnp.tanh(0.7978845608028654 * (x + 0.044715 * (x * x * x))))


def _norm_proj_kernel(x_ref, g_ref, w_ref, o_ref, h_ref):
    @pl.when(pl.program_id(1) == 0)
    def _():
        x = x_ref[...]
        ms = jnp.mean(x * x, axis=-1, keepdims=True)
        h_ref[...] = (x * lax.rsqrt(ms + EPS) * g_ref[...]).astype(bf16)

    o_ref[...] = jnp.dot(h_ref[...], w_ref[...], preferred_element_type=f32).astype(o_ref.dtype)


def _norm_proj(x2d, g, w, tm, tn):
    t, d = x2d.shape
    n = w.shape[1]
    return pl.pallas_call(
        _norm_proj_kernel,
        out_shape=jax.ShapeDtypeStruct((t, n), bf16),
        grid=(t // tm, n // tn),
        in_specs=[pl.BlockSpec((tm, d), lambda i, j: (i, 0)),
                  pl.BlockSpec((1, d), lambda i, j: (0, 0)),
                  pl.BlockSpec((d, tn), lambda i, j: (0, j))],
        out_specs=pl.BlockSpec((tm, tn), lambda i, j: (i, j)),
        scratch_shapes=[pltpu.VMEM((tm, d), bf16)],
        compiler_params=_cparams(("parallel", "arbitrary")),
        name="norm_in_proj",
    )(x2d, g, w)


def _ssm_weights(a_re, a_im, log_dt, b_re, b_im, c_re, c_im):
    hi = lax.Precision.HIGHEST
    g, n = a_re.shape
    c = b_re.shape[2]
    gl = g // SSM_SG
    dt = jnp.exp(log_dt)[:, None]
    lam_re = jnp.minimum(a_re, LAMBDA_RE_MAX)
    lam_im = a_im
    mag = jnp.exp(lam_re * dt)
    ang = lam_im * dt
    lb_re = mag * jnp.cos(ang)
    lb_im = mag * jnp.sin(ang)
    den = lam_re * lam_re + lam_im * lam_im
    num_re = lb_re - 1.0
    coef_re = (num_re * lam_re + lb_im * lam_im) / den
    coef_im = (lb_im * lam_re - num_re * lam_im) / den
    bb_re = coef_re[..., None] * b_re - coef_im[..., None] * b_im
    bb_im = coef_re[..., None] * b_im + coef_im[..., None] * b_re
    pr, pi = [jnp.ones_like(lb_re)], [jnp.zeros_like(lb_re)]
    for _ in range(SSM_L):
        pr_n = pr[-1] * lb_re - pi[-1] * lb_im
        pi_n = pr[-1] * lb_im + pi[-1] * lb_re
        pr.append(pr_n)
        pi.append(pi_n)
    pw_re = jnp.stack(pr)
    pw_im = jnp.stack(pi)
    eye = jnp.eye(gl, dtype=f32)

    xb_re = pw_re[:SSM_L, :, :, None] * bb_re[None] - pw_im[:SSM_L, :, :, None] * bb_im[None]
    xb_im = pw_re[:SSM_L, :, :, None] * bb_im[None] + pw_im[:SSM_L, :, :, None] * bb_re[None]

    k = (jnp.einsum('gcn,lgnd->lgdc', c_re, xb_re, precision=hi)
         - jnp.einsum('gcn,lgnd->lgdc', c_im, xb_im, precision=hi))
    kbd = jnp.einsum('lsgdc,gh->lsgdhc', k.reshape(SSM_L, SSM_SG, gl, c, c), eye)
    kbd = kbd.reshape(SSM_L, SSM_SG, SSM_SG_CH, SSM_SG_CH)
    kbd = jnp.concatenate([kbd, jnp.zeros_like(kbd[:1])], axis=0)
    jj = jnp.arange(SSM_L)[:, None]
    tt = jnp.arange(SSM_L)[None, :]
    lag = jnp.where(tt >= jj, tt - jj, SSM_L)
    m = kbd[lag]
    m = m.transpose(2, 0, 3, 1, 4).reshape(SSM_SG, SSM_ROW, SSM_ROW)

    def state_proj(xb):
        xr = xb[::-1].reshape(SSM_L, SSM_SG, gl, n, c).transpose(1, 0, 2, 4, 3)
        return jnp.einsum('sjgcn,gh->sjgchn', xr, eye)
    p = jnp.stack([state_proj(xb_re), state_proj(xb_im)], axis=4)
    p = p.reshape(SSM_SG, SSM_ROW, 2 * SSM_HALF)

    w_re = pw_re[1:]
    w_im = pw_im[1:]
    q_re = c_re[None] * w_re[:, :, None, :] - c_im[None] * w_im[:, :, None, :]
    q_im = -(c_re[None] * w_im[:, :, None, :] + c_im[None] * w_re[:, :, None, :])

    def out_proj(qp):
        qr = qp.reshape(SSM_L, SSM_SG, gl, c, n).transpose(1, 2, 4, 0, 3)
        return jnp.einsum('sgntc,gh->sgnthc', qr, eye)
    q = jnp.stack([out_proj(q_re), out_proj(q_im)], axis=1)
    q = q.reshape(SSM_SG, 2 * SSM_HALF, SSM_ROW)

    w1 = jnp.concatenate([m, p], axis=2).astype(bf16)
    al_re = pw_re[SSM_L].reshape(SSM_SG, 1, SSM_HALF)
    al_im = pw_im[SSM_L].reshape(SSM_SG, 1, SSM_HALF)
    return w1, q.astype(bf16), al_re, al_im


def _ssm_a_kernel(u_ref, w_ref, yi_ref, sl_ref):
    r = jnp.dot(u_ref[...], w_ref[...], preferred_element_type=f32)
    yi_ref[...] = r[:, :SSM_ROW]
    sl_ref[...] = r[:, SSM_ROW:]


def _ssm_scan_kernel(sl_ref, ar_ref, ai_ref, sp_ref, st_ref):
    nb, tr, _ = sl_ref.shape

    @pl.when(pl.program_id(1) == 0)
    def _():
        st_ref[...] = jnp.zeros(st_ref.shape, f32)

    ar = ar_ref[...]
    ai = ai_ref[...]

    def body(k, carry):
        new = []
        for b in range(nb):
            sr, si = carry[b]
            sp_ref[b, pl.ds(k, 1), :] = jnp.concatenate([sr, si], axis=1)
            row = sl_ref[b, pl.ds(k, 1), :]
            br = row[:, :SSM_HALF]
            bi = row[:, SSM_HALF:]
            new.append((ar * sr - ai * si + br, ar * si + ai * sr + bi))
        return tuple(new)

    init = tuple((st_ref[b][:, :SSM_HALF], st_ref[b][:, SSM_HALF:]) for b in range(nb))
    fin = lax.fori_loop(0, tr, body, init)
    for b in range(nb):
        st_ref[b] = jnp.concatenate(fin[b], axis=1)


def _ssm_b_kernel(yi_ref, sp_ref, u_ref, q_ref, d_ref, y_ref):
    y = (yi_ref[...]
         + jnp.dot(sp_ref[...].astype(bf16), q_ref[...], preferred_element_type=f32)
         + d_ref[...] * u_ref[...].astype(f32))
    y_ref[...] = _gelu_tanh(y).astype(y_ref.dtype)


def _s5_mixer(u2d, batch, w1, q, al_re, al_im, d):
    t = u2d.shape[0]
    r = t // SSM_L
    rb = r // batch
    tr = min(512, rb)
    u_sg = u2d.reshape(r, SSM_L, SSM_SG, SSM_SG_CH).transpose(2, 0, 1, 3).reshape(SSM_SG, r, SSM_ROW)
    yi, sl = pl.pallas_call(
        _ssm_a_kernel,
        out_shape=(jax.ShapeDtypeStruct((SSM_SG, r, SSM_ROW), f32),
                   jax.ShapeDtypeStruct((SSM_SG, r, 2 * SSM_HALF), f32)),
        grid=(SSM_SG, r // tr),
        in_specs=[pl.BlockSpec((None, tr, SSM_ROW), lambda s, i: (s, i, 0)),
                  pl.BlockSpec((None, SSM_ROW, SSM_ROW + 2 * SSM_HALF), lambda s, i: (s, 0, 0))],
        out_specs=(pl.BlockSpec((None, tr, SSM_ROW), lambda s, i: (s, i, 0)),
                   pl.BlockSpec((None, tr, 2 * SSM_HALF), lambda s, i: (s, i, 0))),
        compiler_params=_cparams(("parallel", "parallel")),
        name="ssm_intra",
    )(u_sg, w1)

    sl4 = sl.reshape(SSM_SG, batch, rb, 2 * SSM_HALF)
    sp4 = pl.pallas_call(
        _ssm_scan_kernel,
        out_shape=jax.ShapeDtypeStruct(sl4.shape, f32),
        grid=(SSM_SG, rb // tr),
        in_specs=[pl.BlockSpec((None, batch, tr, 2 * SSM_HALF), lambda s, i: (s, 0, i, 0)),
                  pl.BlockSpec((None, 1, SSM_HALF), lambda s, i: (s, 0, 0)),
                  pl.BlockSpec((None, 1, SSM_HALF), lambda s, i: (s, 0, 0))],
        out_specs=pl.BlockSpec((None, batch, tr, 2 * SSM_HALF), lambda s, i: (s, 0, i, 0)),
        scratch_shapes=[pltpu.VMEM((batch, 1, 2 * SSM_HALF), f32)],
        compiler_params=_cparams(("parallel", "arbitrary")),
        name="ssm_scan",
    )(sl4, al_re, al_im)
    sp = sp4.reshape(SSM_SG, r, 2 * SSM_HALF)

    d_sg = jnp.tile(d.reshape(SSM_SG, 1, SSM_SG_CH), (1, 1, SSM_L))
    y_sg = pl.pallas_call(
        _ssm_b_kernel,
        out_shape=jax.ShapeDtypeStruct((SSM_SG, r, SSM_ROW), bf16),
        grid=(SSM_SG, r // tr),
        in_specs=[pl.BlockSpec((None, tr, SSM_ROW), lambda s, i: (s, i, 0)),
                  pl.BlockSpec((None, tr, 2 * SSM_HALF), lambda s, i: (s, i, 0)),
                  pl.BlockSpec((None, tr, SSM_ROW), lambda s, i: (s, i, 0)),
                  pl.BlockSpec((None, 2 * SSM_HALF, SSM_ROW), lambda s, i: (s, 0, 0)),
                  pl.BlockSpec((None, 1, SSM_ROW), lambda s, i: (s, 0, 0))],
        out_specs=pl.BlockSpec((None, tr, SSM_ROW), lambda s, i: (s, i, 0)),
        compiler_params=_cparams(("parallel", "parallel")),
        name="ssm_inter",
    )(yi, sp, u_sg, q, d_sg)
    return y_sg.reshape(SSM_SG, r, SSM_L, SSM_SG_CH).transpose(1, 2, 0, 3).reshape(t, D_SSM)


def _mla_prep_kernel(ql_ref, kvp_ref, qg_ref, kvg_ref, wq_ref, wkv_ref, cq_ref, sq_ref, ck_ref, sk_ref,
                     gqn_ref, gkn_ref, q_ref, kt_ref, v_ref):
    tm = ql_ref.shape[0]
    lane = lax.broadcasted_iota(jnp.int32, (tm, LANES), 1)
    rope_lanes = lane < QK_ROPE

    ql = ql_ref[...].astype(f32)
    hq = (ql * lax.rsqrt(jnp.mean(ql * ql, axis=-1, keepdims=True) + EPS) * qg_ref[...]).astype(bf16)
    kvp = kvp_ref[...].astype(f32)
    kv = kvp[:, :KV_LORA]
    hkv = (kv * lax.rsqrt(jnp.mean(kv * kv, axis=-1, keepdims=True) + EPS) * kvg_ref[...]).astype(bf16)
    kx = kvp[:, KV_LORA:KV_LORA + LANES]
    kss = jnp.sum(jnp.where(rope_lanes, kx * kx, 0.0), axis=-1, keepdims=True)
    ro = kx * ck_ref[...] + pltpu.roll(kx, QK_ROPE, 1) * sk_ref[...]

    for h in range(MLA_HEADS):
        qx = jnp.dot(hq, wq_ref[h], preferred_element_type=f32)
        x1 = qx[:, :QK_NOPE]
        x2 = qx[:, QK_NOPE:]
        ssq = (jnp.sum(x1 * x1, axis=-1, keepdims=True)
               + jnp.sum(jnp.where(rope_lanes, x2 * x2, 0.0), axis=-1, keepdims=True))
        rinv = lax.rsqrt(ssq * (1.0 / QK_DIM) + EPS) * (QK_DIM ** -0.5 * LOG2_E)
        q_ref[h, :, :QK_NOPE] = (x1 * gqn_ref[...] * rinv).astype(q_ref.dtype)
        q_ref[h, :, QK_NOPE:] = ((x2 * cq_ref[...] + pltpu.roll(x2, QK_ROPE, 1) * sq_ref[...])
                                 * rinv).astype(q_ref.dtype)

        kvx = jnp.dot(hkv, wkv_ref[h], preferred_element_type=f32)
        kn = kvx[:, :QK_NOPE]
        rk = lax.rsqrt((jnp.sum(kn * kn, axis=-1, keepdims=True) + kss) * (1.0 / QK_DIM) + EPS)
        k = jnp.concatenate([kn * gkn_ref[...] * rk, ro * rk], axis=1)
        kt_ref[h] = k.T.astype(kt_ref.dtype)
        v_ref[h] = kvx[:, QK_NOPE:].astype(v_ref.dtype)


def _attn_kernel(q_ref, kt_ref, v_ref, o_ref, s_ref, m_ref, l_ref, acc_ref, *, tq, sub):
    qi = pl.program_id(2)
    m_ref[...] = jnp.full(m_ref.shape, -jnp.inf, f32)
    l_ref[...] = jnp.zeros(l_ref.shape, f32)
    acc_ref[...] = jnp.zeros(acc_ref.shape, f32)

    def scores(j):
        start = pl.multiple_of(j * tq, tq)
        return jnp.dot(q_ref[...], kt_ref[:, pl.ds(start, tq)], preferred_element_type=f32)

    def update(i, s, v):
        rows = pl.ds(i * sub, sub)
        m_prev = m_ref[rows, :]
        m_new = jnp.maximum(m_prev, jnp.max(s, axis=-1, keepdims=True))
        alpha = jnp.exp2(m_prev - m_new)
        p = jnp.exp2(s - m_new)
        l_ref[rows, :] = alpha * l_ref[rows, :] + jnp.sum(p, axis=-1, keepdims=True)
        acc_ref[rows, :] = alpha * acc_ref[rows, :] + jnp.dot(p.astype(v.dtype), v, preferred_element_type=f32)
        m_ref[rows, :] = m_new

    s_ref[...] = scores(0)

    def full_body(j, c):
        s_next = scores(j + 1)
        v = v_ref[pl.ds(pl.multiple_of(j * tq, tq), tq), :]
        for i in range(tq // sub):
            update(i, s_ref[pl.ds(i * sub, sub), :], v)
        s_ref[...] = s_next
        return c

    lax.fori_loop(0, qi, full_body, 0)

    start = pl.multiple_of(qi * tq, tq)
    for i in range(tq // sub):
        nk = (i + 1) * sub
        s = s_ref[pl.ds(i * sub, sub), pl.ds(0, nk)]
        q_chunk = (i * sub + lax.broadcasted_iota(jnp.int32, (sub, nk), 0)) // CHUNK
        k_chunk = lax.broadcasted_iota(jnp.int32, (sub, nk), 1) // CHUNK
        update(i, jnp.where(k_chunk <= q_chunk, s, -jnp.inf), v_ref[pl.ds(start, nk), :])
    o_ref[...] = (acc_ref[...] / l_ref[...]).astype(o_ref.dtype)


def _mla_mixer(proj, positions, q_lat_g, w_q_up, kv_lat_g, w_kv_up, q_norm_g, k_norm_g, batch, seq):
    t = proj.shape[0]
    half = QK_ROPE // 2
    inv = ROPE_THETA ** (-jnp.arange(half, dtype=f32) * (2.0 / QK_ROPE))
    ang = positions.reshape(t).astype(f32)[:, None] * inv
    cos, sin = jnp.cos(ang), jnp.sin(ang)
    zero = jnp.zeros((t, 2 * half), f32)

    def rope_tables(gain):
        g1, g2 = gain[QK_NOPE:QK_NOPE + half], gain[QK_NOPE + half:]
        return (jnp.concatenate([g1 * cos, g2 * cos, zero], axis=1),
                jnp.concatenate([-g2 * sin, g1 * sin, zero], axis=1))

    cq, sq = rope_tables(q_norm_g)
    ck, sk = rope_tables(k_norm_g)
    wq = w_q_up.reshape(Q_LORA, MLA_HEADS, QK_DIM).transpose(1, 0, 2)
    r1, r2 = wq[..., QK_NOPE:QK_NOPE + half], wq[..., QK_NOPE + half:]
    wq = jnp.concatenate([wq, r2, r1], axis=-1).astype(bf16)
    wkv = w_kv_up.reshape(KV_LORA, MLA_HEADS, QK_NOPE + V_HEAD).transpose(1, 0, 2).astype(bf16)

    tm = min(512, seq)
    nsb = seq // tm
    row = lambda i: (i, 0)
    const = lambda i: (0, 0)
    const3 = lambda i: (0, 0, 0)
    head_o = lambda i: (i // nsb, 0, i % nsb, 0)
    q, kt, v = pl.pallas_call(
        _mla_prep_kernel,
        out_shape=(jax.ShapeDtypeStruct((batch, MLA_HEADS, seq, 2 * LANES), bf16),
                   jax.ShapeDtypeStruct((batch, MLA_HEADS, 2 * LANES, seq), bf16),
                   jax.ShapeDtypeStruct((batch, MLA_HEADS, seq, V_HEAD), bf16)),
        grid=(t // tm,),
        in_specs=[pl.BlockSpec((tm, Q_LORA), lambda i: (i, OFF_QL // Q_LORA)),
                  pl.BlockSpec((tm, 512), lambda i: (i, OFF_KV // 512)),
                  pl.BlockSpec((1, Q_LORA), const),
                  pl.BlockSpec((1, KV_LORA), const),
                  pl.BlockSpec(wq.shape, const3),
                  pl.BlockSpec(wkv.shape, const3),
                  pl.BlockSpec((tm, LANES), row), pl.BlockSpec((tm, LANES), row),
                  pl.BlockSpec((tm, LANES), row), pl.BlockSpec((tm, LANES), row),
                  pl.BlockSpec((1, QK_NOPE), const), pl.BlockSpec((1, QK_NOPE), const)],
        out_specs=(pl.BlockSpec((None, MLA_HEADS, tm, 2 * LANES), head_o),
                   pl.BlockSpec((None, MLA_HEADS, 2 * LANES, tm), lambda i: (i // nsb, 0, 0, i % nsb)),
                   pl.BlockSpec((None, MLA_HEADS, tm, V_HEAD), head_o)),
        compiler_params=_cparams(("parallel",)),
        name="mla_prep",
    )(proj, proj, q_lat_g.reshape(1, -1), kv_lat_g.reshape(1, -1), wq, wkv, cq, sq, ck, sk,
      q_norm_g[:QK_NOPE].reshape(1, -1), k_norm_g[:QK_NOPE].reshape(1, -1))

    tq = min(512, seq)
    o = pl.pallas_call(
        functools.partial(_attn_kernel, tq=tq, sub=min(512, tq)),
        out_shape=jax.ShapeDtypeStruct((batch, seq, MLA_HEADS * V_HEAD), bf16),
        grid=(batch, MLA_HEADS, seq // tq),
        in_specs=[pl.BlockSpec((None, None, tq, 2 * LANES), lambda b, h, i: (b, h, i, 0)),
                  pl.BlockSpec((None, None, 2 * LANES, seq), lambda b, h, i: (b, h, 0, 0)),
                  pl.BlockSpec((None, None, seq, V_HEAD), lambda b, h, i: (b, h, 0, 0))],
        out_specs=pl.BlockSpec((None, tq, V_HEAD), lambda b, h, i: (b, i, h)),
        scratch_shapes=[pltpu.VMEM((tq, tq), f32), pltpu.VMEM((tq, 1), f32), pltpu.VMEM((tq, 1), f32),
                        pltpu.VMEM((tq, V_HEAD), f32)],
        compiler_params=_cparams(("parallel", "parallel", "arbitrary")),
        name="mla_attention",
    )(q, kt, v)
    return o.reshape(t, MLA_HEADS * V_HEAD)


def _mix_kernel(y_ref, gs_ref, o_ref, gm_ref, x_ref, wv_ref, wg_ref, wp_ref, wo_ref, out_ref):
    y = y_ref[...]
    val = jnp.dot(y, wv_ref[...], preferred_element_type=f32)
    gate = jnp.dot(y, wg_ref[...], preferred_element_type=f32)
    ssm = val * jax.nn.sigmoid(gate) * jax.nn.sigmoid(gs_ref[...].astype(f32))
    mla = jnp.dot(o_ref[...], wp_ref[...], preferred_element_type=f32) * jax.nn.sigmoid(gm_ref[...].astype(f32))
    merged = (ssm + mla).astype(bf16)
    out_ref[...] = x_ref[...] + jnp.dot(merged, wo_ref[...], preferred_element_type=f32)


def _mix(y, proj, o, x2d, wv, wg, wp, wo, tm):
    t = x2d.shape[0]
    row = lambda i: (i, 0)
    const = lambda i: (0, 0)
    resident = lambda shape: pl.BlockSpec(shape, const, pipeline_mode=pl.Buffered(1))
    return pl.pallas_call(
        _mix_kernel,
        out_shape=jax.ShapeDtypeStruct((t, D_MODEL), f32),
        grid=(t // tm,),
        in_specs=[pl.BlockSpec((tm, D_SSM), row),
                  pl.BlockSpec((tm, D_MODEL), lambda i: (i, OFF_GS // D_MODEL)),
                  pl.BlockSpec((tm, MLA_HEADS * V_HEAD), row),
                  pl.BlockSpec((tm, D_MODEL), lambda i: (i, OFF_GM // D_MODEL)),
                  pl.BlockSpec((tm, D_MODEL), row),
                  resident(wv.shape), resident(wg.shape), resident(wp.shape), resident(wo.shape)],
        out_specs=pl.BlockSpec((tm, D_MODEL), row),
        compiler_params=_cparams(("parallel",)),
        name="merge_out_proj",
    )(y, proj, o, proj, x2d, wv, wg, wp, wo)


def _moe_pre_kernel(x_ref, g_ref, wr_ref, rb_ref, wsg_ref, wsu_ref, wsd_ref, h2_ref, base_ref, te_ref, tw_ref):
    x = x_ref[...]
    tm = x.shape[0]
    h2 = x * lax.rsqrt(jnp.mean(x * x, axis=-1, keepdims=True) + EPS) * g_ref[...]
    for j in range(ROW_TILES):
        h2_ref[pl.ds(j, tm, stride=ROW_TILES), :] = h2[:, j * LANES:(j + 1) * LANES]
    hb = h2.astype(bf16)
    hid = (jax.nn.silu(jnp.dot(hb, wsg_ref[...], preferred_element_type=f32))
           * jnp.dot(hb, wsu_ref[...], preferred_element_type=f32))
    base_ref[...] = x + jnp.dot(hid.astype(bf16), wsd_ref[...], preferred_element_type=f32)

    logits = jnp.dot(h2, wr_ref[...], preferred_element_type=f32, precision=lax.Precision.HIGHEST)
    scores = jax.nn.sigmoid(logits)
    sel = scores + rb_ref[...]
    neg = -jnp.inf
    per_grp = N_EXPERTS // N_EXPERT_GROUPS
    lane = lax.broadcasted_iota(jnp.int32, (tm, N_EXPERTS), 1).astype(f32)
    grp = jnp.floor(lane * (1.0 / per_grp))

    def first_max(vals, ids, sentinel):
        mx = jnp.max(vals, axis=-1, keepdims=True)
        return mx, jnp.min(jnp.where(vals == mx, ids, sentinel), axis=-1, keepdims=True)

    gscore = jnp.zeros((tm, N_EXPERTS), f32)
    for g in range(N_EXPERT_GROUPS):
        in_g = grp == float(g)
        mg = jnp.where(in_g, sel, neg)
        m1, i1 = first_max(mg, lane, float(N_EXPERTS))
        m2 = jnp.max(jnp.where(lane == i1, neg, mg), axis=-1, keepdims=True)
        gscore = jnp.where(in_g, m1 + m2, gscore)
    allowed = jnp.zeros((tm, N_EXPERTS), f32)
    for _ in range(TOPK_GROUPS):
        _, gi = first_max(gscore, grp, float(N_EXPERT_GROUPS))
        hit = grp == gi
        allowed = jnp.where(hit, 1.0, allowed)
        gscore = jnp.where(hit, neg, gscore)
    masked = jnp.where(allowed > 0.0, sel, neg)
    out_lane = lax.broadcasted_iota(jnp.int32, (tm, LANES), 1)
    te = jnp.zeros((tm, LANES), f32)
    tw = jnp.zeros((tm, LANES), f32)
    wsum = jnp.zeros((tm, 1), f32)
    for kk in range(TOP_K):
        _, idx = first_max(masked, lane, float(N_EXPERTS))
        hit = lane == idx
        wk = jnp.sum(jnp.where(hit, scores, 0.0), axis=-1, keepdims=True)
        masked = jnp.where(hit, neg, masked)
        te = jnp.where(out_lane == kk, idx, te)
        tw = jnp.where(out_lane == kk, wk, tw)
        wsum = wsum + wk
    te_ref[...] = te.astype(jnp.int32)
    tw_ref[...] = tw / wsum * ROUTED_SCALE


def _moe_pre(x1, g2, w_router, router_bias, wsg, wsu, wsd, tm):
    t = x1.shape[0]
    row = lambda i: (i, 0)
    const = lambda i: (0, 0)
    return pl.pallas_call(
        _moe_pre_kernel,
        out_shape=(jax.ShapeDtypeStruct((t * ROW_TILES, LANES), f32),
                   jax.ShapeDtypeStruct((t, D_MODEL), f32),
                   jax.ShapeDtypeStruct((t, LANES), jnp.int32),
                   jax.ShapeDtypeStruct((t, LANES), f32)),
        grid=(t // tm,),
        in_specs=[pl.BlockSpec((tm, D_MODEL), row),
                  pl.BlockSpec((1, D_MODEL), const),
                  pl.BlockSpec(w_router.shape, const),
                  pl.BlockSpec((1, N_EXPERTS), const),
                  pl.BlockSpec(wsg.shape, const), pl.BlockSpec(wsu.shape, const), pl.BlockSpec(wsd.shape, const)],
        out_specs=(pl.BlockSpec((tm * ROW_TILES, LANES), row),
                   pl.BlockSpec((tm, D_MODEL), row),
                   pl.BlockSpec((tm, LANES), row),
                   pl.BlockSpec((tm, LANES), row)),
        compiler_params=_cparams(("parallel",)),
        name="norm2_router_shared",
    )(x1, g2, w_router, router_bias, wsg, wsu, wsd)


def _routing_tables(top_e, n_blk):
    t = top_e.shape[0]
    onehot = (top_e[:, :, None] == jnp.arange(N_EXPERTS, dtype=jnp.int32)).astype(jnp.int32)
    per_tok = jnp.sum(onehot, axis=1)
    incl = jnp.cumsum(per_tok, axis=0)
    excl = incl - per_tok
    counts = incl[-1]
    padded = (counts + MOE_BM - 1) // MOE_BM * MOE_BM
    pad_end = jnp.cumsum(padded)
    pad_start = pad_end - padded
    pos = jnp.sum(onehot * (excl + pad_start)[:, None, :], axis=-1).reshape(t * TOP_K)
    blk_start = jnp.arange(n_blk, dtype=jnp.int32) * MOE_BM
    blk_expert = jnp.minimum(jnp.sum((pad_end[None, :] <= blk_start[:, None]).astype(jnp.int32), axis=1),
                             N_EXPERTS - 1)

    n_slots = n_blk * MOE_BM
    n_free = n_slots - t * TOP_K
    free_cnt = padded - counts
    free_end = jnp.cumsum(free_cnt)
    j = jnp.arange(n_free, dtype=jnp.int32)
    owner = (free_end[None, :] <= j[:, None]).astype(jnp.int32)
    e_j = jnp.sum(owner, axis=1)
    own = (e_j[:, None] == jnp.arange(N_EXPERTS, dtype=jnp.int32)).astype(jnp.int32)
    in_expert = jnp.sum(own * (pad_start + counts - (free_end - free_cnt))[None, :], axis=1) + j
    free_slot = jnp.where(e_j < N_EXPERTS, in_expert, pad_end[-1] + j - free_end[-1])
    keys = jnp.concatenate([pos, free_slot]).astype(jnp.int32)
    toks = jnp.concatenate([jnp.arange(t * TOP_K, dtype=jnp.int32) // TOP_K, jnp.zeros((n_free,), jnp.int32)])
    _, slot_tok = lax.sort((keys, toks), num_keys=1)
    return pos.astype(jnp.int32), slot_tok.reshape(n_blk, 1, MOE_BM), blk_expert.astype(jnp.int32)


def _slab_copy(src_ref, src_row, dst_ref, dst_row, sem):
    def rows(r):
        start = r * ROW_TILES
        return pl.ds(start if isinstance(r, int) else pl.multiple_of(start, ROW_TILES), ROW_TILES)
    return pltpu.make_async_copy(src_ref.at[rows(src_row), :], dst_ref.at[rows(dst_row), :], sem)


def _slab_matrix(ref, n, lead=()):
    return jnp.concatenate([ref[lead + (pl.ds(j, n, stride=ROW_TILES), slice(None))] for j in range(ROW_TILES)],
                           axis=1)


def _expert_kernel(be_ref, tok0_ref, tok1_ref, tok2_ref, h2_ref, wga_ref, wua_ref, wda_ref, wgb_ref, wub_ref,
                   wdb_ref, y_ref, xa_ref, xb_ref, sem):
    del be_ref
    i = pl.program_id(0)

    def gather(t_ref, buf, s):
        def body(g, c):
            for u in range(8):
                r = g * 8 + u
                _slab_copy(h2_ref, t_ref[0, r], buf, r, sem.at[s]).start(priority=u % 2)
            return c
        lax.fori_loop(0, MOE_BM // 8, body, 0)

    def wait(buf, s):
        for _ in range(MOE_BM):
            _slab_copy(h2_ref, 0, buf, 0, sem.at[s]).wait()

    def experts(buf, wg_ref, wu_ref, wd_ref, half):
        x = _slab_matrix(buf, MOE_BM).astype(bf16)
        hid = (jax.nn.silu(jnp.dot(x, wg_ref[...], preferred_element_type=f32))
               * jnp.dot(x, wu_ref[...], preferred_element_type=f32))
        y_ref[half * MOE_BM:(half + 1) * MOE_BM, :] = jnp.dot(hid.astype(bf16), wd_ref[...],
                                                             preferred_element_type=f32)

    @pl.when(i == 0)
    def _():
        gather(tok0_ref, xa_ref, 0)

    wait(xa_ref, 0)
    gather(tok1_ref, xb_ref, 1)
    experts(xa_ref, wga_ref, wua_ref, wda_ref, 0)
    wait(xb_ref, 1)
    gather(tok2_ref, xa_ref, 0)
    experts(xb_ref, wgb_ref, wub_ref, wdb_ref, 1)

    @pl.when(i == pl.num_programs(0) - 1)
    def _():
        wait(xa_ref, 0)


def _combine_kernel(pos_ref, pos_next_ref, base_ref, w_ref, ys_ref, out_ref, buf_ref, sem):
    tc = base_ref.shape[0]
    i = pl.program_id(0)
    slot = i % 2

    def row_copy(src_row, kk, dst_row, s):
        return pltpu.make_async_copy(ys_ref.at[pl.ds(src_row, 1), :], buf_ref.at[s, kk, pl.ds(dst_row, 1), :],
                                     sem.at[s])

    def gather(p_ref, s):
        def body(r, c):
            for kk in range(TOP_K):
                row_copy(p_ref[r * TOP_K + kk], kk, r, s).start(priority=kk % 2)
            return c
        lax.fori_loop(0, tc, body, 0)

    def wait(s):
        def body(r, c):
            for _ in range(TOP_K):
                row_copy(0, 0, 0, s).wait()
            return c
        lax.fori_loop(0, tc, body, 0)

    @pl.when(i == 0)
    def _():
        gather(pos_ref, 0)

    gather(pos_next_ref, 1 - slot)
    wait(slot)
    w = w_ref[...]
    acc = base_ref[...]
    for kk in range(TOP_K):
        acc = acc + w[:, kk:kk + 1] * buf_ref[slot, kk]
    out_ref[...] = acc

    @pl.when(i == pl.num_programs(0) - 1)
    def _():
        wait(1 - slot)


def _routed_experts(h2, base, top_e, top_w, weg, weu, wed):
    t = base.shape[0]
    n_assign = t * TOP_K
    n_blk = -(-n_assign // MOE_BM) + N_EXPERTS
    n_blk += n_blk % 2
    pos, slot_tok, blk_expert = _routing_tables(top_e[:, :TOP_K], n_blk)

    def tok_spec(blk):
        return pl.BlockSpec((None, 1, MOE_BM), lambda i, be: (blk(i), 0, 0), memory_space=pltpu.SMEM)

    def w_spec(shape, half):
        return pl.BlockSpec((None,) + shape, lambda i, be: (be[2 * i + half], 0, 0))

    slab_rows = MOE_BM * ROW_TILES
    ys = pl.pallas_call(
        _expert_kernel,
        out_shape=jax.ShapeDtypeStruct((n_blk * MOE_BM, D_MODEL), f32),
        grid_spec=pltpu.PrefetchScalarGridSpec(
            num_scalar_prefetch=1,
            grid=(n_blk // 2,),
            in_specs=[tok_spec(lambda i: 0), tok_spec(lambda i: 2 * i + 1),
                      tok_spec(lambda i: jnp.minimum(2 * i + 2, n_blk - 1)),
                      pl.BlockSpec(memory_space=pl.ANY),
                      w_spec((D_MODEL, D_EXPERT), 0), w_spec((D_MODEL, D_EXPERT), 0), w_spec((D_EXPERT, D_MODEL), 0),
                      w_spec((D_MODEL, D_EXPERT), 1), w_spec((D_MODEL, D_EXPERT), 1), w_spec((D_EXPERT, D_MODEL), 1)],
            out_specs=pl.BlockSpec((2 * MOE_BM, D_MODEL), lambda i, be: (i, 0)),
            scratch_shapes=[pltpu.VMEM((slab_rows, LANES), f32), pltpu.VMEM((slab_rows, LANES), f32),
                            pltpu.SemaphoreType.DMA((2,))]),
        compiler_params=_cparams(("arbitrary",)),
        name="moe_experts",
    )(blk_expert, slot_tok, slot_tok, slot_tok, h2, weg, weu, wed, weg, weu, wed)

    tc = 128
    n_tiles = t // tc
    pos_spec = lambda nxt: pl.BlockSpec((tc * TOP_K,), lambda i: (jnp.minimum(i + nxt, n_tiles - 1),),
                                        memory_space=pltpu.SMEM)
    return pl.pallas_call(
        _combine_kernel,
        out_shape=jax.ShapeDtypeStruct((t, D_MODEL), f32),
        grid=(n_tiles,),
        in_specs=[pos_spec(0), pos_spec(1),
                  pl.BlockSpec((tc, D_MODEL), lambda i: (i, 0)),
                  pl.BlockSpec((tc, LANES), lambda i: (i, 0)),
                  pl.BlockSpec(memory_space=pl.ANY)],
        out_specs=pl.BlockSpec((tc, D_MODEL), lambda i: (i, 0)),
        scratch_shapes=[pltpu.VMEM((2, TOP_K, tc, D_MODEL), f32), pltpu.SemaphoreType.DMA((2,))],
        compiler_params=_cparams(("arbitrary",)),
        name="moe_combine",
    )(pos, pos, base, top_w, ys)


def _in_proj_weight(w_in):
    u, ql, kv, kpe, gs, gm = jnp.split(
        w_in, (D_SSM, D_SSM + Q_LORA, D_SSM + Q_LORA + KV_LORA, D_SSM + Q_LORA + KV_LORA + QK_ROPE,
               D_SSM + Q_LORA + KV_LORA + QK_ROPE + D_MODEL), axis=1)
    half = QK_ROPE // 2
    kpe_ext = jnp.concatenate([kpe, kpe[:, half:], kpe[:, :half]], axis=1)
    pad = jnp.zeros((w_in.shape[0], N_PROJ - OFF_KPE - 2 * QK_ROPE), w_in.dtype)
    return jnp.concatenate([gs, gm, u, ql, kv, kpe_ext, pad], axis=1).astype(bf16)


def _layer(x, positions, norm1_g, w_in, q_lat_g, w_q_up, kv_lat_g, w_kv_up, q_norm_g, k_norm_g,
           ssm_a_re, ssm_a_im, ssm_log_dt, ssm_b_re, ssm_b_im, ssm_c_re, ssm_c_im, ssm_d,
           w_ssm_val, w_ssm_gate, w_mla_proj, w_out, norm2_g, w_router, router_bias,
           w_exp_gate, w_exp_up, w_exp_down, w_sh_gate, w_sh_up, w_sh_down):
    batch, seq, d = x.shape
    t = batch * seq
    x2d = x.reshape(t, d)
    tm_big = min(1024, t)
    proj = _norm_proj(x2d, norm1_g.reshape(1, d), _in_proj_weight(w_in), tm_big, 1024)

    w1, q, al_re, al_im = _ssm_weights(ssm_a_re, ssm_a_im, ssm_log_dt, ssm_b_re, ssm_b_im, ssm_c_re, ssm_c_im)
    y_ssm = _s5_mixer(proj[:, OFF_U:OFF_U + D_SSM], batch, w1, q, al_re, al_im, ssm_d)

    o = _mla_mixer(proj, positions, q_lat_g, w_q_up, kv_lat_g, w_kv_up, q_norm_g, k_norm_g, batch, seq)

    x1 = _mix(y_ssm, proj, o, x2d, w_ssm_val.astype(bf16), w_ssm_gate.astype(bf16),
              w_mla_proj.astype(bf16), w_out.astype(bf16), min(256, t))

    h2, base, top_e, top_w = _moe_pre(x1, norm2_g.reshape(1, d), w_router, router_bias.reshape(1, -1),
                                      w_sh_gate.astype(bf16), w_sh_up.astype(bf16), w_sh_down.astype(bf16),
                                      min(256, t))
    out = _routed_experts(h2, base, top_e, top_w, w_exp_gate.astype(bf16), w_exp_up.astype(bf16),
                          w_exp_down.astype(bf16))
    return out.reshape(batch, seq, d)


def kernel(x, positions, norm1_g, w_in, q_lat_g, w_q_up, kv_lat_g, w_kv_up, q_norm_g, k_norm_g, ssm_a_re, ssm_a_im, ssm_log_dt, ssm_b_re, ssm_b_im, ssm_c_re, ssm_c_im, ssm_d, w_ssm_val, w_ssm_gate, w_mla_proj, w_out, norm2_g, w_router, router_bias, w_exp_gate, w_exp_up, w_exp_down, w_sh_gate, w_sh_up, w_sh_down):
    layer_params = (norm1_g, w_in, q_lat_g, w_q_up, kv_lat_g, w_kv_up, q_norm_g, k_norm_g,
                    ssm_a_re, ssm_a_im, ssm_log_dt, ssm_b_re, ssm_b_im, ssm_c_re, ssm_c_im, ssm_d,
                    w_ssm_val, w_ssm_gate, w_mla_proj, w_out, norm2_g, w_router, router_bias,
                    w_exp_gate, w_exp_up, w_exp_down, w_sh_gate, w_sh_up, w_sh_down)
    for layer in range(norm1_g.shape[0]):
        x = _layer(x, positions, *[p[layer] for p in layer_params])
    return x
```
